```python
import math
import jax, jax.numpy as jnp
from jax import lax
import numpy as np

D_MODEL = 1024
BATCH = 16
SEQ = 2048
DEPTH = 2

PLE_DIM = 256
ROPE_THETA = 500000.0
MAX_POS_OFFSET = 4096
A_HEAD_DIM = 64
A_HEADS = D_MODEL // (2 * A_HEAD_DIM)
A_ROT = A_HEAD_DIM // 4
A_PATTERNS = ((128, 1), (512, 4), (2048, 16))
B_HEADS = D_MODEL // 128
B_Q_LORA = 3 * D_MODEL // 8
B_KV_LORA = D_MODEL // 4
B_NOPE = 64
B_ROPE = 32
B_V = 64
C_HEAD_DIM = 64
C_HEADS = D_MODEL // (2 * C_HEAD_DIM)
C_ROT = C_HEAD_DIM // 4
D_FF = 256 * math.ceil(8 * D_MODEL / 3 / 256)
CONV_WIDTH = 3
Q_BLOCK = 128
LN_EPS = 1e-5
RMS_EPS = 1e-6
NEG_INF = -1e30
N_AB = (DEPTH + 1) // 2
N_C = DEPTH // 2
ALPHA = (2 * DEPTH) ** 0.25
BETA = (8 * DEPTH) ** -0.25
A_WIDTH = A_HEADS * A_HEAD_DIM
AB_IN = 3 * A_WIDTH + B_Q_LORA + B_KV_LORA + B_ROPE
AB_MIX = A_WIDTH + B_HEADS * B_V
C_QK = C_HEADS * 2 * C_HEAD_DIM
C_MIX = C_HEADS * 2 * C_HEAD_DIM

kernel_name = "hybrid_dilated_mla_diff_encoder"


def layer_norm(x, g, b):
    xf = x.astype(jnp.float32)
    mu = jnp.mean(xf, axis=-1, keepdims=True)
    var = jnp.mean(jnp.square(xf - mu), axis=-1, keepdims=True)
    y = (xf - mu) * lax.rsqrt(var + LN_EPS)
    return (y * g.astype(jnp.float32) + b.astype(jnp.float32)).astype(x.dtype)


def rms_norm(x, g, eps):
    xf = x.astype(jnp.float32)
    y = xf * lax.rsqrt(jnp.mean(jnp.square(xf), axis=-1, keepdims=True) + eps)
    return (y * g.astype(jnp.float32)).astype(x.dtype)


def rope_tables(positions, rot):
    inv_freq = 1.0 / (ROPE_THETA ** (jnp.arange(0, rot, 2, dtype=jnp.float32) / rot))
    ang = positions.astype(jnp.float32)[..., None] * inv_freq
    return jnp.cos(ang), jnp.sin(ang)


def apply_rope(x, cos, sin, rot):
    bshape = cos.shape[:2] + (1,) * (x.ndim - 3) + cos.shape[-1:]
    c = cos.reshape(bshape).astype(x.dtype)
    s = sin.reshape(bshape).astype(x.dtype)
    half = rot // 2
    x1, x2, rest = x[..., :half], x[..., half:rot], x[..., rot:]
    return jnp.concatenate([x1 * c - x2 * s, x2 * c + x1 * s, rest], axis=-1)


def dilated_window_attention(q, k, v, window, dilation):
    B, H, S, dh = q.shape
    n_side = window // (2 * dilation)
    L = S // dilation
    blk = n_side
    nb = -(-L // blk)
    Lp = nb * blk

    def strided(t):
        return t.reshape(B, H, L, dilation, dh).transpose(0, 1, 3, 2, 4)

    def slabs(t):
        tp = jnp.pad(t, ((0, 0), (0, 0), (0, 0), (blk, Lp - L + blk), (0, 0)))
        tp = tp.reshape(B, H, dilation, nb + 2, blk, dh)
        return jnp.concatenate([tp[:, :, :, 0:nb], tp[:, :, :, 1:nb + 1], tp[:, :, :, 2:nb + 2]], axis=4)

    qb = jnp.pad(strided(q), ((0, 0), (0, 0), (0, 0), (0, Lp - L), (0, 0))).reshape(B, H, dilation, nb, blk, dh)
    kb = slabs(strided(k))
    vb = slabs(strided(v))
    s = jnp.einsum("bhrnqd,bhrnkd->bhrnqk", qb, kb) * (dh ** -0.5)
    qi = jnp.arange(nb)[:, None, None] * blk + jnp.arange(blk)[None, :, None]
    ki = jnp.arange(nb)[:, None, None] * blk - blk + jnp.arange(3 * blk)[None, None, :]
    valid = (ki >= 0) & (ki < L) & (jnp.abs(ki - qi) <= n_side)
    s = jnp.where(valid, s, NEG_INF)
    m = jnp.max(s, axis=-1, keepdims=True)
    e = jnp.exp(s - m)
    l = jnp.sum(e, axis=-1)
    o = jnp.einsum("bhrnqk,bhrnkd->bhrnqd", e, vb) / l[..., None]
    lse = m[..., 0] + jnp.log(l)
    o = o.reshape(B, H, dilation, Lp, dh)[:, :, :, :L].transpose(0, 1, 3, 2, 4).reshape(B, H, S, dh)
    lse = lse.reshape(B, H, dilation, Lp)[..., :L].transpose(0, 1, 3, 2).reshape(B, H, S)
    return o, lse


def dilated_mixture(q, k, v):
    B, S, H, dh = q.shape
    qf, kf, vf = (t.astype(jnp.float32).transpose(0, 2, 1, 3) for t in (q, k, v))
    outs, lses = [], []
    for window, dilation in A_PATTERNS:
        o, lse = dilated_window_attention(qf, kf, vf, window, dilation)
        outs.append(o)
        lses.append(lse)
    w = jax.nn.softmax(jnp.stack(lses), axis=0)
    o = jnp.sum(w[..., None] * jnp.stack(outs), axis=0)
    return o.transpose(0, 2, 1, 3).reshape(B, S, H * dh)


def dense_attention(q, k, v, scale):
    B, S, H, dq = q.shape
    dv = v.shape[-1]
    nq = S // Q_BLOCK
    kf = k.astype(jnp.float32).transpose(0, 2, 1, 3)
    vf = v.astype(jnp.float32).transpose(0, 2, 1, 3)
    qb = q.astype(jnp.float32).transpose(0, 2, 1, 3).reshape(B, H, nq, Q_BLOCK, dq).transpose(2, 0, 1, 3, 4)

    def block(qi):
        s = jnp.einsum("bhqd,bhkd->bhqk", qi, kf) * scale
        return jnp.einsum("bhqk,bhkd->bhqd", jax.nn.softmax(s, axis=-1), vf)

    o = lax.map(block, qb)
    return o.transpose(1, 0, 3, 2, 4).reshape(B, S, H * dv)


def diff_attention(q, k, v, lam):
    B, S, H, _, dh = q.shape
    dv = v.shape[-1]
    nq = S // Q_BLOCK
    kf = k.astype(jnp.float32).transpose(0, 2, 3, 1, 4)
    vf = v.astype(jnp.float32).transpose(0, 2, 1, 3)
    qb = q.astype(jnp.float32).transpose(0, 2, 3, 1, 4).reshape(B, H, 2, nq, Q_BLOCK, dh).transpose(3, 0, 1, 2, 4, 5)

    def block(qi):
        s = jnp.einsum("bhcqd,bhckd->bhcqk", qi, kf) * (dh ** -0.5)
        a = jax.nn.softmax(s, axis=-1)
        return jnp.einsum("bhqk,bhkd->bhqd", a[:, :, 0] - lam * a[:, :, 1], vf)

    o = lax.map(block, qb)
    return o.transpose(1, 0, 3, 2, 4).reshape(B, S, H, dv)


def mixer_ab(x, cos_a, sin_a, cos_b, sin_b, w_in, q_norm, w_q_up, kv_norm, w_kv_up, w_out):
    B, S, _ = x.shape
    h = x @ w_in
    o1 = A_WIDTH
    o2 = 2 * A_WIDTH
    o3 = 3 * A_WIDTH
    o4 = o3 + B_Q_LORA
    o5 = o4 + B_KV_LORA
    shape_a = (B, S, A_HEADS, A_HEAD_DIM)
    qa = apply_rope(h[..., :o1].reshape(shape_a), cos_a, sin_a, A_ROT)
    ka = apply_rope(h[..., o1:o2].reshape(shape_a), cos_a, sin_a, A_ROT)
    va = h[..., o2:o3].reshape(shape_a)
    out_a = dilated_mixture(qa, ka, va).astype(x.dtype)
    cq = rms_norm(h[..., o3:o4], q_norm, RMS_EPS)
    qb = (cq @ w_q_up).reshape(B, S, B_HEADS, B_NOPE + B_ROPE)
    q_pe = apply_rope(qb[..., B_NOPE:], cos_b, sin_b, B_ROPE)
    qb = jnp.concatenate([qb[..., :B_NOPE], q_pe], axis=-1)
    ckv = rms_norm(h[..., o4:o5], kv_norm, RMS_EPS)
    kv = (ckv @ w_kv_up).reshape(B, S, B_HEADS, B_NOPE + B_V)
    k_pe = apply_rope(h[..., o5:].reshape(B, S, 1, B_ROPE), cos_b, sin_b, B_ROPE)
    kb = jnp.concatenate([kv[..., :B_NOPE], jnp.broadcast_to(k_pe, (B, S, B_HEADS, B_ROPE))], axis=-1)
    vb = kv[..., B_NOPE:]
    out_b = dense_attention(qb, kb, vb, (B_NOPE + B_ROPE) ** -0.5).astype(x.dtype)
    return jnp.concatenate([out_a, out_b], axis=-1) @ w_out


def mixer_c(x, cos_c, sin_c, w_qkv, lam_params, subln, w_out, lambda_init):
    B, S, _ = x.shape
    h = x @ w_qkv
    q = apply_rope(h[..., :C_QK].reshape(B, S, C_HEADS, 2, C_HEAD_DIM), cos_c, sin_c, C_ROT)
    k = apply_rope(h[..., C_QK:2 * C_QK].reshape(B, S, C_HEADS, 2, C_HEAD_DIM), cos_c, sin_c, C_ROT)
    v = h[..., 2 * C_QK:].reshape(B, S, C_HEADS, 2 * C_HEAD_DIM)
    lp = lam_params.astype(jnp.float32)
    lam = jnp.exp(jnp.sum(lp[0] * lp[1])) - jnp.exp(jnp.sum(lp[2] * lp[3])) + lambda_init
    o = diff_attention(q, k, v, lam)
    o = rms_norm(o, subln, LN_EPS) * (1.0 - lambda_init)
    return o.reshape(B, S, C_MIX).astype(x.dtype) @ w_out


def conv_ffn(x, w_gate, w_up, conv_w, conv_b, w_down):
    S = x.shape[1]
    a = x @ w_gate
    u = x @ w_up
    pad = CONV_WIDTH // 2
    ap = jnp.pad(a, ((0, 0), (pad, pad), (0, 0)))
    c = conv_b
    for j in range(CONV_WIDTH):
        c = c + conv_w[j] * ap[:, j:j + S]
    return (jax.nn.gelu(c) * u) @ w_down


def setup_inputs(seed: int = 0) -> dict:
    key = jax.random.key(seed)
    ks = jax.random.split(key, 24)
    f32 = jnp.float32

    def w(k, shape, fan_in, gain=1.0):
        return jax.random.normal(k, shape, f32) * (gain * fan_in ** -0.5)

    def gain_init(k, shape):
        return 1.0 + 0.02 * jax.random.normal(k, shape, f32)

    def bias_init(k, shape):
        return 0.02 * jax.random.normal(k, shape, f32)

    x = jax.random.normal(ks[0], (BATCH, SEQ, D_MODEL), f32)
    p = jax.random.normal(ks[1], (DEPTH, BATCH, SEQ, PLE_DIM), f32)
    positions = (jnp.arange(SEQ, dtype=jnp.int32)[None, :]
                 + jax.random.randint(ks[2], (BATCH, 1), 0, MAX_POS_OFFSET, dtype=jnp.int32))
    return {
        "x": x,
        "p": p,
        "positions": positions,
        "ab_w_in": w(ks[3], (N_AB, D_MODEL, AB_IN), D_MODEL),
        "ab_q_norm": gain_init(ks[4], (N_AB, B_Q_LORA)),
        "ab_w_q_up": w(ks[5], (N_AB, B_Q_LORA, B_HEADS * (B_NOPE + B_ROPE)), B_Q_LORA),
        "ab_kv_norm": gain_init(ks[6], (N_AB, B_KV_LORA)),
        "ab_w_kv_up": w(ks[7], (N_AB, B_KV_LORA, B_HEADS * (B_NOPE + B_V)), B_KV_LORA),
        "ab_w_out": w(ks[8], (N_AB, AB_MIX, D_MODEL), AB_MIX, BETA),
        "c_w_qkv": w(ks[9], (N_C, D_MODEL, 2 * C_QK + C_MIX), D_MODEL),
        "c_lambda": 0.1 * jax.random.normal(ks[10], (N_C, 4, C_HEAD_DIM), f32),
        "c_subln": gain_init(ks[11], (N_C, 2 * C_HEAD_DIM)),
        "c_w_out": w(ks[12], (N_C, C_MIX, D_MODEL), C_MIX, BETA),
        "ln_mix_g": gain_init(ks[13], (DEPTH, D_MODEL)),
        "ln_mix_b": bias_init(ks[14], (DEPTH, D_MODEL)),
        "ffn_w_gate": w(ks[15], (DEPTH, D_MODEL, D_FF), D_MODEL),
        "ffn_w_up": w(ks[16], (DEPTH, D_MODEL, D_FF), D_MODEL),
        "ffn_conv_w": w(ks[17], (DEPTH, CONV_WIDTH, D_FF), CONV_WIDTH),
        "ffn_conv_b": bias_init(ks[18], (DEPTH, D_FF)),
        "ffn_w_down": w(ks[19], (DEPTH, D_FF, D_MODEL), D_FF, BETA),
        "ln_ffn_g": gain_init(ks[20], (DEPTH, D_MODEL)),
        "ln_ffn_b": bias_init(ks[21], (DEPTH, D_MODEL)),
        "ple_w_gate": w(ks[22], (DEPTH, D_MODEL, D_MODEL), D_MODEL),
        "ple_w_proj": w(ks[23], (DEPTH, PLE_DIM, D_MODEL), PLE_DIM),
    }


def reference(x, p, positions, ab_w_in, ab_q_norm, ab_w_q_up, ab_kv_norm, ab_w_kv_up, ab_w_out,
              c_w_qkv, c_lambda, c_subln, c_w_out, ln_mix_g, ln_mix_b, ffn_w_gate, ffn_w_up,
              ffn_conv_w, ffn_conv_b, ffn_w_down, ln_ffn_g, ln_ffn_b, ple_w_gate, ple_w_proj):
    cos_a, sin_a = rope_tables(positions, A_ROT)
    cos_b, sin_b = rope_tables(positions, B_ROPE)
    cos_c, sin_c = rope_tables(positions, C_ROT)
    for i in range(DEPTH):
        j = i // 2
        if i % 2 == 0:
            y = mixer_ab(x, cos_a, sin_a, cos_b, sin_b, ab_w_in[j], ab_q_norm[j], ab_w_q_up[j],
                         ab_kv_norm[j], ab_w_kv_up[j], ab_w_out[j])
        else:
            lambda_init = 0.8 - 0.6 * math.exp(-0.3 * i)
            y = mixer_c(x, cos_c, sin_c, c_w_qkv[j], c_lambda[j], c_subln[j], c_w_out[j], lambda_init)
        x = layer_norm(ALPHA * x + y, ln_mix_g[i], ln_mix_b[i])
        f = conv_ffn(x, ffn_w_gate[i], ffn_w_up[i], ffn_conv_w[i], ffn_conv_b[i], ffn_w_down[i])
        x = layer_norm(ALPHA * x + f, ln_ffn_g[i], ln_ffn_b[i])
        x = x + jax.nn.sigmoid(x @ ple_w_gate[i]) * (p[i] @ ple_w_proj[i])
    return x
```

```python
import functools
import math

import jax
import jax.numpy as jnp
from jax import lax
from jax.experimental import pallas as pl
from jax.experimental.pallas import tpu as pltpu

F32 = jnp.float32
BF16 = jnp.bfloat16

D_MODEL = 1024
DEPTH = 2
PLE_DIM = 256
ROPE_THETA = 500000.0
A_HEAD_DIM = 64
A_HEADS = 8
A_ROT = 16
A_PATTERNS = ((128, 1), (512, 4), (2048, 16))
A_SIDE = 64
B_HEADS = 8
B_Q_LORA = 384
B_KV_LORA = 256
B_NOPE = 64
B_ROPE = 32
B_V = 64
C_HEAD_DIM = 64
C_HEADS = 8
C_ROT = 16
D_FF = 2816
LN_EPS = 1e-5
RMS_EPS = 1e-6
NEG_INF = -1e30
ALPHA = (2 * DEPTH) ** 0.25
A_WIDTH = A_HEADS * A_HEAD_DIM
LOG2E = 1.4426950408889634

LANES = 128
VMEM_LIMIT = 56 * 1024 * 1024

PROJ_ROWS = 256
FFN_ROWS = 512
FFN_CHUNK = 256
HALO = 8
ATT_Q_ROWS = 256
DIL_Q_ROWS = 128


def _params(n_axes):
    return pltpu.CompilerParams(
        dimension_semantics=("arbitrary",) * n_axes, vmem_limit_bytes=VMEM_LIMIT)


def _dot(a, b):
    return jnp.dot(a, b, preferred_element_type=F32)


def _dot_nt(a, b):
    return lax.dot_general(a, b, (((1,), (1,)), ((), ())), preferred_element_type=F32)


def _rope_group(x, cos_t, sin_t, half, take_upper):
    upper = pltpu.roll(x, LANES - half, axis=1)
    lower = pltpu.roll(x, half, axis=1)
    return x * cos_t + jnp.where(take_upper, upper, lower) * sin_t


def _rope_wide(x, cos_t, sin_t, half, take_upper):
    groups = [
        _rope_group(x[:, g * LANES:(g + 1) * LANES], cos_t, sin_t, half, take_upper)
        for g in range(x.shape[1] // LANES)
    ]
    return jnp.concatenate(groups, axis=1)


def _rms_norm(h, gain, eps):
    ms = jnp.mean(h * h, axis=-1, keepdims=True)
    return h * lax.rsqrt(ms + eps) * gain


def _layer_norm(z, gain, bias):
    mu = jnp.mean(z, axis=-1, keepdims=True)
    zc = z - mu
    var = jnp.mean(zc * zc, axis=-1, keepdims=True)
    return zc * lax.rsqrt(var + LN_EPS) * gain + bias


def _lane_iota():
    return lax.broadcasted_iota(jnp.int32, (1, LANES), 1)


def _ab_in_kernel(x_ref, wall_ref, qn_ref, wq_ref, kvn_ref, wkv_ref, ca_ref, sa_ref, cb_ref,
                  sb_ref, qa_ref, ka_ref, va_ref, qb_ref, kb_ref, vb_ref):
    lane = _lane_iota()
    upper_a = (lane % A_HEAD_DIM) < (A_ROT // 2)
    upper_b = (lane >= B_NOPE) & (lane < B_NOPE + B_ROPE // 2)
    xb = x_ref[...].astype(BF16)
    h = _dot(xb, wall_ref[...])
    ca, sa = ca_ref[...], sa_ref[...]
    cb, sb = cb_ref[...], sb_ref[...]
    o1, o2, o3 = A_WIDTH, 2 * A_WIDTH, 3 * A_WIDTH
    o4 = o3 + B_Q_LORA
    o5 = o4 + B_KV_LORA
    qa_ref[...] = _rope_wide(h[:, :o1], ca, sa, A_ROT // 2, upper_a)
    ka_ref[...] = _rope_wide(h[:, o1:o2], ca, sa, A_ROT // 2, upper_a)
    va_ref[...] = h[:, o2:o3]
    cq = _rms_norm(h[:, o3:o4], qn_ref[...], RMS_EPS).astype(BF16)
    qb = _dot(cq, wq_ref[...])
    qb_ref[...] = _rope_wide(qb, cb, sb, B_ROPE // 2, upper_b).astype(BF16)
    ckv = _rms_norm(h[:, o4:o5], kvn_ref[...], RMS_EPS).astype(BF16)
    kv = _dot(ckv, wkv_ref[...])
    kpe = _rope_group(h[:, o5:o5 + LANES], cb, sb, B_ROPE // 2, upper_b)
    kslots = B_HEADS * LANES
    kb_ref[...] = (kv[:, :kslots] + jnp.concatenate([kpe] * B_HEADS, axis=1)).astype(BF16)
    vb_ref[...] = kv[:, kslots:].astype(BF16)


def _ab_in_proj(x2, wall, qn, wq, kvn, wkv, ca, sa, cb, sb):
    m = x2.shape[0]
    tm = PROJ_ROWS
    row = lambda w: pl.BlockSpec((tm, w), lambda i: (i, 0))
    full = lambda a: pl.BlockSpec(a.shape, lambda i: (0, 0))
    out_shape = (
        jax.ShapeDtypeStruct((m, A_WIDTH), F32),
        jax.ShapeDtypeStruct((m, A_WIDTH), F32),
        jax.ShapeDtypeStruct((m, A_WIDTH), F32),
        jax.ShapeDtypeStruct((m, B_HEADS * LANES), BF16),
        jax.ShapeDtypeStruct((m, B_HEADS * LANES), BF16),
        jax.ShapeDtypeStruct((m, B_HEADS * B_V), BF16),
    )
    return pl.pallas_call(
        _ab_in_kernel,
        grid=(m // tm,),
        in_specs=[row(D_MODEL), full(wall), full(qn), full(wq), full(kvn), full(wkv),
                  row(LANES), row(LANES), row(LANES), row(LANES)],
        out_specs=(row(A_WIDTH), row(A_WIDTH), row(A_WIDTH), row(B_HEADS * LANES),
                   row(B_HEADS * LANES), row(B_HEADS * B_V)),
        out_shape=out_shape,
        compiler_params=_params(1),
        name="ab_in_proj",
    )(x2, wall, qn, wq, kvn, wkv, ca, sa, cb, sb)


def _dilated_kernel(q_ref, k_ref, v_ref, o_ref, num_ref, m_ref, l_ref):
    seq = q_ref.shape[0]
    tq = DIL_Q_ROWS
    c = (A_HEAD_DIM ** -0.5) * LOG2E
    head0 = _lane_iota() < A_HEAD_DIM

    def block(dil, tk, res, q0, first):
        length = seq // dil
        ws = jnp.clip(q0 - A_SIDE, 0, length - tk)
        if dil == 1:
            qrows, krows = pl.ds(q0, tq), pl.ds(ws, tk)
        else:
            qrows = pl.ds(res + q0 * dil, tq, stride=dil)
            krows = pl.ds(res + ws * dil, tk, stride=dil)
        q = q_ref[qrows, :]
        k = k_ref[krows, :].astype(BF16)
        v = v_ref[krows, :].astype(BF16)
        delta = (lax.broadcasted_iota(jnp.int32, (tq, tk), 1)
                 - lax.broadcasted_iota(jnp.int32, (tq, tk), 0) + (ws - q0))
        valid = jnp.abs(delta) <= A_SIDE
        stats = []
        for hm in (head0, jnp.logical_not(head0)):
            s = _dot_nt(jnp.where(hm, q, 0.0).astype(BF16), k)
            s = jnp.where(valid, s, NEG_INF)
            m = jnp.max(s, axis=-1, keepdims=True)
            p = jnp.exp2((s - m) * c)
            l = jnp.sum(p, axis=-1, keepdims=True)
            stats.append((m, l, _dot(p.astype(BF16), v)))
        (m0, l0, pv0), (m1, l1, pv1) = stats
        m_new = jnp.where(head0, m0, m1)
        l_new = jnp.where(head0, l0, l1)
        num_new = jnp.where(head0, pv0, pv1)
        if not first:
            m_old, l_old, num_old = m_ref[qrows, :], l_ref[qrows, :], num_ref[qrows, :]
            m_all = jnp.maximum(m_old, m_new)
            a_old = jnp.exp2((m_old - m_all) * c)
            a_new = jnp.exp2((m_new - m_all) * c)
            num_new = num_old * a_old + num_new * a_new
            l_new = l_old * a_old + l_new * a_new
            m_new = m_all
        m_ref[qrows, :] = m_new
        l_ref[qrows, :] = l_new
        num_ref[qrows, :] = num_new

    for idx, (window, dil) in enumerate(A_PATTERNS):
        assert window // (2 * dil) == A_SIDE
        length = seq // dil
        tk = min(tq + 2 * A_SIDE, length)
        blocks_per_res = length // tq
        shift = blocks_per_res.bit_length() - 1
        assert blocks_per_res == 1 << shift

        def body(i, carry, dil=dil, tk=tk, shift=shift, blocks_per_res=blocks_per_res, first=idx == 0):
            res = lax.shift_right_logical(i, shift)
            qb = lax.bitwise_and(i, blocks_per_res - 1)
            block(dil, tk, res, qb * tq, first)
            return carry

        lax.fori_loop(0, dil * blocks_per_res, body, 0)

    o_ref[...] = (num_ref[...] / l_ref[...]).astype(o_ref.dtype)


def _dilated_attention(qa, ka, va, batch, seq):
    q3, k3, v3 = (t.reshape(batch, seq, A_WIDTH) for t in (qa, ka, va))
    spec = pl.BlockSpec((None, seq, LANES), lambda b, j: (b, 0, j))
    out = pl.pallas_call(
        _dilated_kernel,
        grid=(batch, A_WIDTH // LANES),
        in_specs=[spec, spec, spec],
        out_specs=spec,
        out_shape=jax.ShapeDtypeStruct((batch, seq, A_WIDTH), BF16),
        scratch_shapes=[pltpu.VMEM((seq, LANES), F32)] * 3,
        compiler_params=_params(2),
        name="dilated_attention",
    )(q3, k3, v3)
    return out.reshape(batch * seq, A_WIDTH)


def _mla_attn_kernel(q_ref, k_ref, v_ref, o_ref):
    c = ((B_NOPE + B_ROPE) ** -0.5) * LOG2E
    v = v_ref[...]
    outs = []
    for h in range(2):
        q = q_ref[:, h * LANES:(h + 1) * LANES]
        k = k_ref[:, h * LANES:(h + 1) * LANES]
        s = _dot_nt(q, k)
        m = jnp.max(s, axis=-1, keepdims=True)
        p = jnp.exp2((s - m) * c)
        l = jnp.sum(p, axis=-1, keepdims=True)
        outs.append(_dot(p.astype(BF16), v) * (1.0 / l))
    o_ref[...] = jnp.where(_lane_iota() < B_V, outs[0], outs[1]).astype(o_ref.dtype)


def _mla_attention(qb, kb, vb, batch, seq):
    tq = ATT_Q_ROWS
    q3 = qb.reshape(batch, seq, B_HEADS * LANES)
    k3 = kb.reshape(batch, seq, B_HEADS * LANES)
    v3 = vb.reshape(batch, seq, B_HEADS * B_V)
    out = pl.pallas_call(
        _mla_attn_kernel,
        grid=(batch, B_HEADS // 2, seq // tq),
        in_specs=[
            pl.BlockSpec((None, tq, 2 * LANES), lambda b, j, i: (b, i, j)),
            pl.BlockSpec((None, seq, 2 * LANES), lambda b, j, i: (b, 0, j)),
            pl.BlockSpec((None, seq, LANES), lambda b, j, i: (b, 0, j)),
        ],
        out_specs=pl.BlockSpec((None, tq, LANES), lambda b, j, i: (b, i, j)),
        out_shape=jax.ShapeDtypeStruct((batch, seq, B_HEADS * B_V), BF16),
        compiler_params=_params(3),
        name="mla_attention",
    )(q3, k3, v3)
    return out.reshape(batch * seq, B_HEADS * B_V)


def _diff_attn_kernel(lam_ref, subln_ref, q_ref, k_ref, v_ref, o_ref, *, lambda_init):
    c = (C_HEAD_DIM ** -0.5) * LOG2E
    lp = lam_ref[...]
    t1 = jnp.sum(lp[0:1, :] * lp[1:2, :], axis=-1, keepdims=True)
    t2 = jnp.sum(lp[2:3, :] * lp[3:4, :], axis=-1, keepdims=True)
    lam = jnp.exp(t1) - jnp.exp(t2) + lambda_init
    comp0 = _lane_iota() < C_HEAD_DIM
    q = q_ref[...]
    k = k_ref[...]
    zero = jnp.zeros_like(q)
    probs = []
    for cm in (comp0, jnp.logical_not(comp0)):
        s = _dot_nt(jnp.where(cm, q, zero), k)
        m = jnp.max(s, axis=-1, keepdims=True)
        p = jnp.exp2((s - m) * c)
        l = jnp.sum(p, axis=-1, keepdims=True)
        probs.append((p, l))
    (p0, l0), (p1, l1) = probs
    a = p0 * (1.0 / l0) - p1 * (lam / l1)
    o = _dot(a.astype(BF16), v_ref[...])
    o = _rms_norm(o, subln_ref[...], LN_EPS) * (1.0 - lambda_init)
    o_ref[...] = o.astype(o_ref.dtype)


def _diff_attention(lam_params, subln, q, k, v, batch, seq, lambda_init):
    tq = ATT_Q_ROWS
    width = C_HEADS * 2 * C_HEAD_DIM
    q3, k3, v3 = (t.reshape(batch, seq, width) for t in (q, k, v))
    full = lambda a: pl.BlockSpec(a.shape, lambda b, h, i: (0, 0))
    kv_spec = pl.BlockSpec((None, seq, LANES), lambda b, h, i: (b, 0, h))
    q_spec = pl.BlockSpec((None, tq, LANES), lambda b, h, i: (b, i, h))
    out = pl.pallas_call(
        functools.partial(_diff_attn_kernel, lambda_init=lambda_init),
        grid=(batch, C_HEADS, seq // tq),
        in_specs=[full(lam_params), full(subln), q_spec, kv_spec, kv_spec],
        out_specs=q_spec,
        out_shape=jax.ShapeDtypeStruct((batch, seq, width), BF16),
        compiler_params=_params(3),
        name="diff_attention",
    )(lam_params, subln, q3, k3, v3)
    return out.reshape(batch * seq, width)


def _c_in_kernel(x_ref, w_ref, ca_ref, sa_ref, q_ref, k_ref, v_ref):
    upper = (_lane_iota() % C_HEAD_DIM) < (C_ROT // 2)
    width = q_ref.shape[1]
    h = _dot(x_ref[...].astype(BF16), w_ref[...])
    ca, sa = ca_ref[...], sa_ref[...]
    q_ref[...] = _rope_wide(h[:, :width], ca, sa, C_ROT // 2, upper).astype(BF16)
    k_ref[...] = _rope_wide(h[:, width:2 * width], ca, sa, C_ROT // 2, upper).astype(BF16)
    v_ref[...] = h[:, 2 * width:].astype(BF16)


def _c_in_proj(x2, w, ca, sa):
    m = x2.shape[0]
    tm = PROJ_ROWS
    width = w.shape[1] // 3
    row = lambda n: pl.BlockSpec((tm, n), lambda i: (i, 0))
    out = jax.ShapeDtypeStruct((m, width), BF16)
    return pl.pallas_call(
        _c_in_kernel,
        grid=(m // tm,),
        in_specs=[row(D_MODEL), pl.BlockSpec(w.shape, lambda i: (0, 0)), row(LANES), row(LANES)],
        out_specs=(row(width), row(width), row(width)),
        out_shape=(out, out, out),
        compiler_params=_params(1),
        name="c_in_proj",
    )(x2, w, ca, sa)


def _out_proj_ln_kernel(*refs, n_in):
    x_ref = refs[0]
    a_refs = refs[1:1 + n_in]
    w_refs = refs[1 + n_in:1 + 2 * n_in]
    g_ref, b_ref, o_ref = refs[1 + 2 * n_in:]
    y = _dot(a_refs[0][...], w_refs[0][...])
    for a_ref, w_ref in zip(a_refs[1:], w_refs[1:]):
        y = y + _dot(a_ref[...], w_ref[...])
    o_ref[...] = _layer_norm(ALPHA * x_ref[...] + y, g_ref[...], b_ref[...])


def _out_proj_ln(x2, acts, weights, gain, bias):
    m = x2.shape[0]
    tm = PROJ_ROWS
    row = lambda n: pl.BlockSpec((tm, n), lambda i: (i, 0))
    full = lambda a: pl.BlockSpec(a.shape, lambda i: (0, 0))
    return pl.pallas_call(
        functools.partial(_out_proj_ln_kernel, n_in=len(acts)),
        grid=(m // tm,),
        in_specs=([row(D_MODEL)] + [row(a.shape[1]) for a in acts] + [full(w) for w in weights]
                  + [full(gain), full(bias)]),
        out_specs=row(D_MODEL),
        out_shape=jax.ShapeDtypeStruct((m, D_MODEL), F32),
        compiler_params=_params(1),
        name="out_proj_ln",
    )(x2, *acts, *weights, gain, bias)


def _gelu_tanh(c):
    return 0.5 * c * (1.0 + jnp.tanh(math.sqrt(2.0 / math.pi) * (c + 0.044715 * (c * c * c))))


def _ffn_kernel(x_ref, prev_ref, next_ref, p_ref, wgu_ref, cw_ref, cb_ref, wd_ref, g_ref, b_ref,
                wpg_ref, wpp_ref, o_ref, g_scr, *, tiles_per_seq):
    tm = x_ref.shape[0]
    n_chunks = wgu_ref.shape[0]
    tn = FFN_CHUNK
    i = pl.program_id(0)
    pos = lax.rem(i, tiles_per_seq)
    x = x_ref[...]
    prev = jnp.where(pos == 0, 0.0, prev_ref[...])
    nxt = jnp.where(pos == tiles_per_seq - 1, 0.0, next_ref[...])
    xb = jnp.concatenate([prev, x, nxt], axis=0).astype(BF16)
    rows = tm + 2 * HALO

    def chunk(n, carry):
        au = _dot(xb, wgu_ref[n])
        a = au[:, :tn]
        u = au[HALO:HALO + tm, tn:]
        cw = cw_ref[n]
        a_prev = pltpu.roll(a, 1, axis=0)[HALO:HALO + tm]
        a_next = pltpu.roll(a, rows - 1, axis=0)[HALO:HALO + tm]
        conv = (cb_ref[n] + cw[0:1, :] * a_prev + cw[1:2, :] * a[HALO:HALO + tm]
                + cw[2:3, :] * a_next)
        g_scr[n] = (_gelu_tanh(conv) * u).astype(BF16)
        return carry

    lax.fori_loop(0, n_chunks, chunk, 0)
    f = _dot(g_scr[0], wd_ref[0])
    for n in range(1, n_chunks):
        f = f + _dot(g_scr[n], wd_ref[n])
    x2 = _layer_norm(ALPHA * x + f, g_ref[...], b_ref[...])
    gate = jax.nn.sigmoid(_dot(x2.astype(BF16), wpg_ref[...]))
    proj = _dot(p_ref[...].astype(BF16), wpp_ref[...])
    o_ref[...] = x2 + gate * proj


def _ffn_block(x2, p2, wgu, cw, cb, wd, gain, bias, wpg, wpp, seq):
    m = x2.shape[0]
    tm = FFN_ROWS
    halo_blocks = tm // HALO
    n_chunks = wgu.shape[0]
    row = lambda n: pl.BlockSpec((tm, n), lambda i: (i, 0))

    def resident(a):
        nd = a.ndim
        return pl.BlockSpec(a.shape, lambda i: (0,) * nd, pipeline_mode=pl.Buffered(1))

    prev_spec = pl.BlockSpec((HALO, D_MODEL), lambda i: (jnp.maximum(i * halo_blocks - 1, 0), 0))
    next_spec = pl.BlockSpec(
        (HALO, D_MODEL), lambda i: (jnp.minimum((i + 1) * halo_blocks, m // HALO - 1), 0))
    return pl.pallas_call(
        functools.partial(_ffn_kernel, tiles_per_seq=seq // tm),
        grid=(m // tm,),
        in_specs=[row(D_MODEL), prev_spec, next_spec, row(PLE_DIM), resident(wgu), resident(cw),
                  resident(cb), resident(wd), resident(gain), resident(bias), resident(wpg),
                  resident(wpp)],
        out_specs=row(D_MODEL),
        out_shape=jax.ShapeDtypeStruct((m, D_MODEL), F32),
        scratch_shapes=[pltpu.VMEM((n_chunks, tm, FFN_CHUNK), BF16)],
        compiler_params=_params(1),
        name="ffn_block",
    )(x2, x2, x2, p2, wgu, cw, cb, wd, gain, bias, wpg, wpp)


def _rope_tables(positions, rot):
    inv_freq = 1.0 / (ROPE_THETA ** (jnp.arange(0, rot, 2, dtype=F32) / rot))
    ang = positions.astype(F32)[..., None] * inv_freq
    return jnp.cos(ang), jnp.sin(ang)


def _lane_tables(positions):
    m = positions.size
    cos_a, sin_a = (t.reshape(m, -1) for t in _rope_tables(positions, A_ROT))
    cos_b, sin_b = (t.reshape(m, -1) for t in _rope_tables(positions, B_ROPE))
    ones = lambda n: jnp.ones((m, n), F32)
    zeros = lambda n: jnp.zeros((m, n), F32)
    rest = A_HEAD_DIM - A_ROT
    ca = jnp.concatenate([cos_a, cos_a, ones(rest)] * 2, axis=1)
    sa = jnp.concatenate([-sin_a, sin_a, zeros(rest)] * 2, axis=1)
    pad = LANES - B_NOPE - B_ROPE
    cb = jnp.concatenate([ones(B_NOPE), cos_b, cos_b, ones(pad)], axis=1)
    sb = jnp.concatenate([zeros(B_NOPE), -sin_b, sin_b, zeros(pad)], axis=1)
    return ca, sa, cb, sb


def _ab_weights(w_in, w_q_up, w_kv_up):
    o3 = 3 * A_WIDTH
    o5 = o3 + B_Q_LORA + B_KV_LORA
    k_pe = jnp.pad(w_in[:, o5:], ((0, 0), (B_NOPE, LANES - B_NOPE - B_ROPE)))
    wall = jnp.concatenate([w_in[:, :o5], k_pe], axis=1).astype(BF16)
    per_head = B_NOPE + B_ROPE
    wq = jnp.pad(w_q_up.reshape(B_Q_LORA, B_HEADS, per_head),
                 ((0, 0), (0, 0), (0, LANES - per_head))).reshape(B_Q_LORA, B_HEADS * LANES)
    kv = w_kv_up.reshape(B_KV_LORA, B_HEADS, B_NOPE + B_V)
    wk = jnp.pad(kv[:, :, :B_NOPE], ((0, 0), (0, 0), (0, LANES - B_NOPE)))
    wkv = jnp.concatenate([wk.reshape(B_KV_LORA, B_HEADS * LANES),
                           kv[:, :, B_NOPE:].reshape(B_KV_LORA, B_HEADS * B_V)], axis=1)
    return wall, wq.astype(BF16), wkv.astype(BF16)


def _ffn_weights(w_gate, w_up, conv_w, conv_b, w_down):
    n_chunks = D_FF // FFN_CHUNK
    chunks = lambda w: w.reshape(w.shape[0], n_chunks, FFN_CHUNK).transpose(1, 0, 2)
    wgu = jnp.concatenate([chunks(w_gate), chunks(w_up)], axis=2).astype(BF16)
    cw = chunks(conv_w)
    cb = conv_b.reshape(n_chunks, 1, FFN_CHUNK)
    wd = w_down.reshape(n_chunks, FFN_CHUNK, D_MODEL).astype(BF16)
    return wgu, cw, cb, wd


def kernel(x, p, positions, ab_w_in, ab_q_norm, ab_w_q_up, ab_kv_norm, ab_w_kv_up, ab_w_out,
           c_w_qkv, c_lambda, c_subln, c_w_out, ln_mix_g, ln_mix_b, ffn_w_gate, ffn_w_up,
           ffn_conv_w, ffn_conv_b, ffn_w_down, ln_ffn_g, ln_ffn_b, ple_w_gate, ple_w_proj):
    batch, seq, d = x.shape
    m = batch * seq
    ca, sa, cb, sb = _lane_tables(positions)
    x2 = x.reshape(m, d)
    row_vec = lambda v: v.reshape(1, -1)
    for i in range(DEPTH):
        j = i // 2
        if i % 2 == 0:
            wall, wq, wkv = _ab_weights(ab_w_in[j], ab_w_q_up[j], ab_w_kv_up[j])
            qa, ka, va, qb, kb, vb = _ab_in_proj(
                x2, wall, row_vec(ab_q_norm[j]), wq, row_vec(ab_kv_norm[j]), wkv, ca, sa, cb, sb)
            out_a = _dilated_attention(qa, ka, va, batch, seq)
            out_b = _mla_attention(qb, kb, vb, batch, seq)
            w_out = ab_w_out[j].astype(BF16)
            acts, weights = [out_a, out_b], [w_out[:A_WIDTH], w_out[A_WIDTH:]]
        else:
            lambda_init = 0.8 - 0.6 * math.exp(-0.3 * i)
            q, k, v = _c_in_proj(x2, c_w_qkv[j].astype(BF16), ca, sa)
            subln = row_vec(c_subln[j])
            acts = [_diff_attention(c_lambda[j], subln, q, k, v, batch, seq, lambda_init)]
            weights = [c_w_out[j].astype(BF16)]
        x2 = _out_proj_ln(x2, acts, weights, row_vec(ln_mix_g[i]), row_vec(ln_mix_b[i]))
        wgu, cw, cbias, wd = _ffn_weights(
            ffn_w_gate[i], ffn_w_up[i], ffn_conv_w[i], ffn_conv_b[i], ffn_w_down[i])
        x2 = _ffn_block(x2, p[i].reshape(m, PLE_DIM), wgu, cw, cbias, wd, row_vec(ln_ffn_g[i]),
                        row_vec(ln_ffn_b[i]), ple_w_gate[i].astype(BF16),
                        ple_w_proj[i].astype(BF16), seq)
    return x2.reshape(batch, seq, d)
```

```python
import functools
import math

import numpy as np
import jax
import jax.numpy as jnp
from jax import lax
from jax.experimental import pallas as pl
from jax.experimental.pallas import tpu as pltpu

F32 = jnp.float32
BF16 = jnp.bfloat16

D_MODEL = 1024
DEPTH = 2
PLE_DIM = 256
ROPE_THETA = 500000.0
A_HEAD_DIM = 64
A_HEADS = 8
A_ROT = 16
A_PATTERNS = ((128, 1), (512, 4), (2048, 16))
A_SIDE = 64
B_HEADS = 8
B_Q_LORA = 384
B_KV_LORA = 256
B_NOPE = 64
B_ROPE = 32
B_V = 64
C_HEAD_DIM = 64
C_HEADS = 8
C_ROT = 16
D_FF = 2816
LN_EPS = 1e-5
RMS_EPS = 1e-6
NEG_INF = -1e30
ALPHA = (2 * DEPTH) ** 0.25
A_WIDTH = A_HEADS * A_HEAD_DIM
LOG2E = 1.4426950408889634

LANES = 128
VMEM_LIMIT = 56 * 1024 * 1024

PROJ_ROWS = 256
FFN_ROWS = 512
FFN_CHUNK = 256
HALO = 8
ATT_Q_ROWS = 256
ATT_UNROLL = 4
DIL_Q_ROWS = 128


def _params(n_axes):
    return pltpu.CompilerParams(
        dimension_semantics=("arbitrary",) * n_axes, vmem_limit_bytes=VMEM_LIMIT)


def _dot(a, b):
    return jnp.dot(a, b, preferred_element_type=F32)


def _dot_nt(a, b):
    return lax.dot_general(a, b, (((1,), (1,)), ((), ())), preferred_element_type=F32)


def _rope_group(x, cos_t, sin_t, half, take_upper):
    upper = pltpu.roll(x, LANES - half, axis=1)
    lower = pltpu.roll(x, half, axis=1)
    return x * cos_t + jnp.where(take_upper, upper, lower) * sin_t


def _rope_wide(x, cos_t, sin_t, half, take_upper):
    groups = [
        _rope_group(x[:, g * LANES:(g + 1) * LANES], cos_t, sin_t, half, take_upper)
        for g in range(x.shape[1] // LANES)
    ]
    return jnp.concatenate(groups, axis=1)


def _rms_norm(h, gain, eps):
    ms = jnp.mean(h * h, axis=-1, keepdims=True)
    return h * lax.rsqrt(ms + eps) * gain


def _layer_norm(z, gain, bias):
    mu = jnp.mean(z, axis=-1, keepdims=True)
    zc = z - mu
    var = jnp.mean(zc * zc, axis=-1, keepdims=True)
    return zc * lax.rsqrt(var + LN_EPS) * gain + bias


def _lane_iota():
    return lax.broadcasted_iota(jnp.int32, (1, LANES), 1)


def _softmax_pv(s, v_aug, c):
    m = jnp.max(s, axis=-1, keepdims=True)
    p = jnp.exp2((s - m) * c).astype(BF16)
    return _dot(p, v_aug), m


def _ab_in_kernel(x_ref, wall_ref, qn_ref, wq_ref, kvn_ref, wkv_ref, ca_ref, sa_ref, cb_ref,
                  sb_ref, qa_ref, ka_ref, va_ref, qb_ref, kb_ref, vb_ref):
    lane = _lane_iota()
    upper_a = (lane % A_HEAD_DIM) < (A_ROT // 2)
    upper_b = (lane >= B_NOPE) & (lane < B_NOPE + B_ROPE // 2)
    xb = x_ref[...].astype(BF16)
    h = _dot(xb, wall_ref[...])
    ca, sa = ca_ref[...], sa_ref[...]
    cb, sb = cb_ref[...], sb_ref[...]
    o1, o2, o3 = A_WIDTH, 2 * A_WIDTH, 3 * A_WIDTH
    o4 = o3 + B_Q_LORA
    o5 = o4 + B_KV_LORA
    qa_ref[...] = _rope_wide(h[:, :o1], ca, sa, A_ROT // 2, upper_a)
    ka_ref[...] = _rope_wide(h[:, o1:o2], ca, sa, A_ROT // 2, upper_a)
    va_ref[...] = h[:, o2:o3]
    cq = _rms_norm(h[:, o3:o4], qn_ref[...], RMS_EPS).astype(BF16)
    qb = _dot(cq, wq_ref[...])
    qb_ref[...] = _rope_wide(qb, cb, sb, B_ROPE // 2, upper_b).astype(BF16)
    ckv = _rms_norm(h[:, o4:o5], kvn_ref[...], RMS_EPS).astype(BF16)
    kv = _dot(ckv, wkv_ref[...])
    kpe = _rope_group(h[:, o5:o5 + LANES], cb, sb, B_ROPE // 2, upper_b)
    kslots = B_HEADS * LANES
    kb_ref[...] = (kv[:, :kslots] + jnp.concatenate([kpe] * B_HEADS, axis=1)).astype(BF16)
    vb_ref[...] = kv[:, kslots:].astype(BF16)


def _ab_in_proj(x2, wall, qn, wq, kvn, wkv, ca, sa, cb, sb):
    m = x2.shape[0]
    tm = PROJ_ROWS
    row = lambda w: pl.BlockSpec((tm, w), lambda i: (i, 0))
    full = lambda a: pl.BlockSpec(a.shape, lambda i: (0, 0))
    out_shape = (
        jax.ShapeDtypeStruct((m, A_WIDTH), F32),
        jax.ShapeDtypeStruct((m, A_WIDTH), F32),
        jax.ShapeDtypeStruct((m, A_WIDTH), F32),
        jax.ShapeDtypeStruct((m, B_HEADS * LANES), BF16),
        jax.ShapeDtypeStruct((m, B_HEADS * LANES), BF16),
        jax.ShapeDtypeStruct((m, B_HEADS * B_V), BF16),
    )
    return pl.pallas_call(
        _ab_in_kernel,
        grid=(m // tm,),
        in_specs=[row(D_MODEL), full(wall), full(qn), full(wq), full(kvn), full(wkv),
                  row(LANES), row(LANES), row(LANES), row(LANES)],
        out_specs=(row(A_WIDTH), row(A_WIDTH), row(A_WIDTH), row(B_HEADS * LANES),
                   row(B_HEADS * LANES), row(B_HEADS * B_V)),
        out_shape=out_shape,
        compiler_params=_params(1),
        name="ab_in_proj",
    )(x2, wall, qn, wq, kvn, wkv, ca, sa, cb, sb)


def _band_bias(tq, tk, offset):
    delta = np.arange(tk)[None, :] - np.arange(tq)[:, None] + offset
    bias = np.where(np.abs(delta) <= A_SIDE, 0.0, NEG_INF).astype(np.float32)
    return np.concatenate([bias, bias], axis=0)


def _window_start(q0, length, tk):
    return min(max(q0 - A_SIDE, 0), length - tk)


def _dilated_kernel(q_ref, k_ref, v_ref, bias_w_ref, bias_s_ref, o_ref, *stat_refs):
    seq = q_ref.shape[0]
    tq = DIL_Q_ROWS
    c = (A_HEAD_DIM ** -0.5) * LOG2E
    head0 = _lane_iota() < A_HEAD_DIM

    for pat, (window, dil) in enumerate(A_PATTERNS):
        assert window // (2 * dil) == A_SIDE
        num_ref, m_ref, l_ref = stat_refs[3 * pat:3 * pat + 3]
        length = seq // dil
        tk = min(tq + 2 * A_SIDE, length)
        ones = jnp.ones((tk, LANES), BF16)
        for res in range(dil):
            for q0 in range(0, length, tq):
                ws = _window_start(q0, length, tk)
                if dil == 1:
                    qrows, krows = pl.ds(q0, tq), pl.ds(ws, tk)
                else:
                    qrows = pl.ds(res + q0 * dil, tq, stride=dil)
                    krows = pl.ds(res + ws * dil, tk, stride=dil)
                if tk == length:
                    bias = bias_s_ref[...]
                else:
                    bias = bias_w_ref[(0, A_SIDE, 2 * A_SIDE).index(q0 - ws)]
                q = q_ref[qrows, :]
                ql = jnp.concatenate(
                    [jnp.where(head0, q, 0.0), jnp.where(head0, 0.0, q)], axis=0).astype(BF16)
                k = k_ref[krows, :].astype(BF16)
                v_aug = jnp.concatenate([v_ref[krows, :].astype(BF16), ones], axis=1)
                r, m = _softmax_pv(_dot_nt(ql, k) + bias, v_aug, c)
                num_ref[qrows, :] = jnp.where(head0, r[:tq, :LANES], r[tq:, :LANES])
                l_ref[qrows, :] = jnp.where(head0, r[:tq, LANES:], r[tq:, LANES:])
                m_ref[qrows, :] = jnp.where(head0, m[:tq], m[tq:])

    rows = 2 * tq
    n_pat = len(A_PATTERNS)
    for r0 in range(0, seq, rows):
        sl = pl.ds(r0, rows)
        ms = [stat_refs[3 * pat + 1][sl, :] for pat in range(n_pat)]
        m_all = functools.reduce(jnp.maximum, ms)
        scale = [jnp.exp2((mp - m_all) * c) for mp in ms]
        num = sum(stat_refs[3 * pat][sl, :] * scale[pat] for pat in range(n_pat))
        den = sum(stat_refs[3 * pat + 2][sl, :] * scale[pat] for pat in range(n_pat))
        o_ref[sl, :] = (num / den).astype(o_ref.dtype)


def _dilated_attention(qa, ka, va, batch, seq):
    tq = DIL_Q_ROWS
    q3, k3, v3 = (t.reshape(batch, seq, A_WIDTH) for t in (qa, ka, va))
    tkw = tq + 2 * A_SIDE
    bias_w = jnp.asarray(np.stack([_band_bias(tq, tkw, -off) for off in (0, A_SIDE, 2 * A_SIDE)]))
    bias_s = jnp.asarray(_band_bias(tq, tq, 0))
    spec = pl.BlockSpec((None, seq, LANES), lambda b, j: (b, 0, j))
    full = lambda a: pl.BlockSpec(a.shape, lambda b, j: (0,) * a.ndim)
    out = pl.pallas_call(
        _dilated_kernel,
        grid=(batch, A_WIDTH // LANES),
        in_specs=[spec, spec, spec, full(bias_w), full(bias_s)],
        out_specs=spec,
        out_shape=jax.ShapeDtypeStruct((batch, seq, A_WIDTH), BF16),
        scratch_shapes=[pltpu.VMEM((seq, LANES), F32)] * (3 * len(A_PATTERNS)),
        compiler_params=_params(2),
        name="dilated_attention",
    )(q3, k3, v3, bias_w, bias_s)
    return out.reshape(batch * seq, A_WIDTH)


def _mla_attn_kernel(q_ref, k_ref, v_ref, o_ref, vaug_ref):
    seq = q_ref.shape[0]
    tq = ATT_Q_ROWS
    c = ((B_NOPE + B_ROPE) ** -0.5) * LOG2E
    head0 = _lane_iota() < B_V
    vaug_ref[:, :LANES] = v_ref[...]
    vaug_ref[:, LANES:] = jnp.ones((seq, LANES), BF16)

    def tile(i, carry):
        rows = pl.ds(pl.multiple_of(i * tq, tq), tq)
        outs = []
        for h in range(2):
            q = q_ref[rows, h * LANES:(h + 1) * LANES]
            k = k_ref[:, h * LANES:(h + 1) * LANES]
            r, _ = _softmax_pv(_dot_nt(q, k), vaug_ref[...], c)
            outs.append(r[:, :LANES] / r[:, LANES:])
        o_ref[rows, :] = jnp.where(head0, outs[0], outs[1]).astype(o_ref.dtype)
        return carry

    lax.fori_loop(0, seq // tq, tile, 0, unroll=ATT_UNROLL)


def _mla_attention(qb, kb, vb, batch, seq):
    q3 = qb.reshape(batch, seq, B_HEADS * LANES)
    k3 = kb.reshape(batch, seq, B_HEADS * LANES)
    v3 = vb.reshape(batch, seq, B_HEADS * B_V)
    qk_spec = pl.BlockSpec((None, seq, 2 * LANES), lambda b, j: (b, 0, j))
    v_spec = pl.BlockSpec((None, seq, LANES), lambda b, j: (b, 0, j))
    out = pl.pallas_call(
        _mla_attn_kernel,
        grid=(batch, B_HEADS // 2),
        in_specs=[qk_spec, qk_spec, v_spec],
        out_specs=v_spec,
        out_shape=jax.ShapeDtypeStruct((batch, seq, B_HEADS * B_V), BF16),
        scratch_shapes=[pltpu.VMEM((seq, 2 * LANES), BF16)],
        compiler_params=_params(2),
        name="mla_attention",
    )(q3, k3, v3)
    return out.reshape(batch * seq, B_HEADS * B_V)


def _diff_attn_kernel(lam_ref, subln_ref, q_ref, k_ref, v_ref, o_ref, *, lambda_init):
    seq = q_ref.shape[0]
    tq = ATT_Q_ROWS
    c = (C_HEAD_DIM ** -0.5) * LOG2E
    lp = lam_ref[...]
    t1 = jnp.sum(lp[0:1, :] * lp[1:2, :], axis=-1, keepdims=True)
    t2 = jnp.sum(lp[2:3, :] * lp[3:4, :], axis=-1, keepdims=True)
    lam = jnp.exp(t1) - jnp.exp(t2) + lambda_init
    comp0 = _lane_iota() < C_HEAD_DIM
    gain = subln_ref[...] * (1.0 - lambda_init)

    def tile(i, carry):
        rows = pl.ds(pl.multiple_of(i * tq, tq), tq)
        q = q_ref[rows, :]
        zero = jnp.zeros_like(q)
        ql = jnp.concatenate([jnp.where(comp0, q, zero), jnp.where(comp0, zero, q)], axis=0)
        s = _dot_nt(ql, k_ref[...])
        p = jnp.exp2((s - jnp.max(s, axis=-1, keepdims=True)) * c)
        l = jnp.sum(p, axis=-1, keepdims=True)
        l0, l1 = l[:tq], l[tq:]
        a = (p[:tq] - (lam * l0 / l1) * p[tq:]).astype(BF16)
        o = _dot(a, v_ref[...]) / l0
        o_ref[rows, :] = _rms_norm(o, gain, LN_EPS).astype(o_ref.dtype)
        return carry

    lax.fori_loop(0, seq // tq, tile, 0, unroll=ATT_UNROLL)


def _diff_attention(lam_params, subln, q, k, v, batch, seq, lambda_init):
    width = C_HEADS * 2 * C_HEAD_DIM
    q3, k3, v3 = (t.reshape(batch, seq, width) for t in (q, k, v))
    full = lambda a: pl.BlockSpec(a.shape, lambda b, h: (0, 0))
    spec = pl.BlockSpec((None, seq, LANES), lambda b, h: (b, 0, h))
    out = pl.pallas_call(
        functools.partial(_diff_attn_kernel, lambda_init=lambda_init),
        grid=(batch, C_HEADS),
        in_specs=[full(lam_params), full(subln), spec, spec, spec],
        out_specs=spec,
        out_shape=jax.ShapeDtypeStruct((batch, seq, width), BF16),
        compiler_params=_params(2),
        name="diff_attention",
    )(lam_params, subln, q3, k3, v3)
    return out.reshape(batch * seq, width)


def _c_in_kernel(x_ref, w_ref, ca_ref, sa_ref, q_ref, k_ref, v_ref):
    upper = (_lane_iota() % C_HEAD_DIM) < (C_ROT // 2)
    width = q_ref.shape[1]
    h = _dot(x_ref[...].astype(BF16), w_ref[...])
    ca, sa = ca_ref[...], sa_ref[...]
    q_ref[...] = _rope_wide(h[:, :width], ca, sa, C_ROT // 2, upper).astype(BF16)
    k_ref[...] = _rope_wide(h[:, width:2 * width], ca, sa, C_ROT // 2, upper).astype(BF16)
    v_ref[...] = h[:, 2 * width:].astype(BF16)


def _c_in_proj(x2, w, ca, sa):
    m = x2.shape[0]
    tm = PROJ_ROWS
    width = w.shape[1] // 3
    row = lambda n: pl.BlockSpec((tm, n), lambda i: (i, 0))
    out = jax.ShapeDtypeStruct((m, width), BF16)
    return pl.pallas_call(
        _c_in_kernel,
        grid=(m // tm,),
        in_specs=[row(D_MODEL), pl.BlockSpec(w.shape, lambda i: (0, 0)), row(LANES), row(LANES)],
        out_specs=(row(width), row(width), row(width)),
        out_shape=(out, out, out),
        compiler_params=_params(1),
        name="c_in_proj",
    )(x2, w, ca, sa)


def _out_proj_ln_kernel(*refs, n_in):
    x_ref = refs[0]
    a_refs = refs[1:1 + n_in]
    w_refs = refs[1 + n_in:1 + 2 * n_in]
    g_ref, b_ref, o_ref = refs[1 + 2 * n_in:]
    y = _dot(a_refs[0][...], w_refs[0][...])
    for a_ref, w_ref in zip(a_refs[1:], w_refs[1:]):
        y = y + _dot(a_ref[...], w_ref[...])
    o_ref[...] = _layer_norm(ALPHA * x_ref[...] + y, g_ref[...], b_ref[...])


def _out_proj_ln(x2, acts, weights, gain, bias):
    m = x2.shape[0]
    tm = PROJ_ROWS
    row = lambda n: pl.BlockSpec((tm, n), lambda i: (i, 0))
    full = lambda a: pl.BlockSpec(a.shape, lambda i: (0, 0))
    return pl.pallas_call(
        functools.partial(_out_proj_ln_kernel, n_in=len(acts)),
        grid=(m // tm,),
        in_specs=([row(D_MODEL)] + [row(a.shape[1]) for a in acts] + [full(w) for w in weights]
                  + [full(gain), full(bias)]),
        out_specs=row(D_MODEL),
        out_shape=jax.ShapeDtypeStruct((m, D_MODEL), F32),
        compiler_params=_params(1),
        name="out_proj_ln",
    )(x2, *acts, *weights, gain, bias)


def _gelu_tanh(c):
    return 0.5 * c * (1.0 + jnp.tanh(math.sqrt(2.0 / math.pi) * (c + 0.044715 * (c * c * c))))


def _ffn_kernel(x_ref, prev_ref, next_ref, p_ref, wgu_ref, cw_ref, cb_ref, wd_ref, g_ref, b_ref,
                wpg_ref, wpp_ref, o_ref, *, tiles_per_seq):
    tm = x_ref.shape[0]
    n_chunks = wgu_ref.shape[0]
    tn = FFN_CHUNK
    i = pl.program_id(0)
    pos = lax.rem(i, tiles_per_seq)
    x = x_ref[...]
    prev = jnp.where(pos == 0, 0.0, prev_ref[...])
    nxt = jnp.where(pos == tiles_per_seq - 1, 0.0, next_ref[...])
    xb = jnp.concatenate([prev, x, nxt], axis=0).astype(BF16)
    rows = tm + 2 * HALO
    f = None
    for n in range(n_chunks):
        au = _dot(xb, wgu_ref[n])
        a = au[:, :tn]
        u = au[HALO:HALO + tm, tn:]
        cw = cw_ref[n]
        a_prev = pltpu.roll(a, 1, axis=0)[HALO:HALO + tm]
        a_next = pltpu.roll(a, rows - 1, axis=0)[HALO:HALO + tm]
        conv = (cb_ref[n] + cw[0:1, :] * a_prev + cw[1:2, :] * a[HALO:HALO + tm]
                + cw[2:3, :] * a_next)
        d = _dot((_gelu_tanh(conv) * u).astype(BF16), wd_ref[n])
        f = d if f is None else f + d
    x2 = _layer_norm(ALPHA * x + f, g_ref[...], b_ref[...])
    gate = jax.nn.sigmoid(_dot(x2.astype(BF16), wpg_ref[...]))
    proj = _dot(p_ref[...].astype(BF16), wpp_ref[...])
    o_ref[...] = x2 + gate * proj


def _ffn_block(x2, p2, wgu, cw, cb, wd, gain, bias, wpg, wpp, seq):
    m = x2.shape[0]
    tm = FFN_ROWS
    halo_blocks = tm // HALO
    row = lambda n: pl.BlockSpec((tm, n), lambda i: (i, 0))

    def resident(a):
        nd = a.ndim
        return pl.BlockSpec(a.shape, lambda i: (0,) * nd, pipeline_mode=pl.Buffered(1))

    prev_spec = pl.BlockSpec((HALO, D_MODEL), lambda i: (jnp.maximum(i * halo_blocks - 1, 0), 0))
    next_spec = pl.BlockSpec(
        (HALO, D_MODEL), lambda i: (jnp.minimum((i + 1) * halo_blocks, m // HALO - 1), 0))
    return pl.pallas_call(
        functools.partial(_ffn_kernel, tiles_per_seq=seq // tm),
        grid=(m // tm,),
        in_specs=[row(D_MODEL), prev_spec, next_spec, row(PLE_DIM), resident(wgu), resident(cw),
                  resident(cb), resident(wd), resident(gain), resident(bias), resident(wpg),
                  resident(wpp)],
        out_specs=row(D_MODEL),
        out_shape=jax.ShapeDtypeStruct((m, D_MODEL), F32),
        compiler_params=_params(1),
        name="ffn_block",
    )(x2, x2, x2, p2, wgu, cw, cb, wd, gain, bias, wpg, wpp)


def _rope_tables(positions, rot):
    inv_freq = 1.0 / (ROPE_THETA ** (jnp.arange(0, rot, 2, dtype=F32) / rot))
    ang = positions.astype(F32)[..., None] * inv_freq
    return jnp.cos(ang), jnp.sin(ang)


def _lane_tables(positions):
    m = positions.size
    cos_a, sin_a = (t.reshape(m, -1) for t in _rope_tables(positions, A_ROT))
    cos_b, sin_b = (t.reshape(m, -1) for t in _rope_tables(positions, B_ROPE))
    ones = lambda n: jnp.ones((m, n), F32)
    zeros = lambda n: jnp.zeros((m, n), F32)
    rest = A_HEAD_DIM - A_ROT
    ca = jnp.concatenate([cos_a, cos_a, ones(rest)] * 2, axis=1)
    sa = jnp.concatenate([-sin_a, sin_a, zeros(rest)] * 2, axis=1)
    pad = LANES - B_NOPE - B_ROPE
    cb = jnp.concatenate([ones(B_NOPE), cos_b, cos_b, ones(pad)], axis=1)
    sb = jnp.concatenate([zeros(B_NOPE), -sin_b, sin_b, zeros(pad)], axis=1)
    return ca, sa, cb, sb


def _ab_weights(w_in, w_q_up, w_kv_up):
    o3 = 3 * A_WIDTH
    o5 = o3 + B_Q_LORA + B_KV_LORA
    k_pe = jnp.pad(w_in[:, o5:], ((0, 0), (B_NOPE, LANES - B_NOPE - B_ROPE)))
    wall = jnp.concatenate([w_in[:, :o5], k_pe], axis=1).astype(BF16)
    per_head = B_NOPE + B_ROPE
    wq = jnp.pad(w_q_up.reshape(B_Q_LORA, B_HEADS, per_head),
                 ((0, 0), (0, 0), (0, LANES - per_head))).reshape(B_Q_LORA, B_HEADS * LANES)
    kv = w_kv_up.reshape(B_KV_LORA, B_HEADS, B_NOPE + B_V)
    wk = jnp.pad(kv[:, :, :B_NOPE], ((0, 0), (0, 0), (0, LANES - B_NOPE)))
    wkv = jnp.concatenate([wk.reshape(B_KV_LORA, B_HEADS * LANES),
                           kv[:, :, B_NOPE:].reshape(B_KV_LORA, B_HEADS * B_V)], axis=1)
    return wall, wq.astype(BF16), wkv.astype(BF16)


def _ffn_weights(w_gate, w_up, conv_w, conv_b, w_down):
    n_chunks = D_FF // FFN_CHUNK
    chunks = lambda w: w.reshape(w.shape[0], n_chunks, FFN_CHUNK).transpose(1, 0, 2)
    wgu = jnp.concatenate([chunks(w_gate), chunks(w_up)], axis=2).astype(BF16)
    cw = chunks(conv_w)
    cb = conv_b.reshape(n_chunks, 1, FFN_CHUNK)
    wd = w_down.reshape(n_chunks, FFN_CHUNK, D_MODEL).astype(BF16)
    return wgu, cw, cb, wd


def kernel(x, p, positions, ab_w_in, ab_q_norm, ab_w_q_up, ab_kv_norm, ab_w_kv_up, ab_w_out,
           c_w_qkv, c_lambda, c_subln, c_w_out, ln_mix_g, ln_mix_b, ffn_w_gate, ffn_w_up,
           ffn_conv_w, ffn_conv_b, ffn_w_down, ln_ffn_g, ln_ffn_b, ple_w_gate, ple_w_proj):
    batch, seq, d = x.shape
    m = batch * seq
    ca, sa, cb, sb = _lane_tables(positions)
    x2 = x.reshape(m, d)
    row_vec = lambda v: v.reshape(1, -1)
    for i in range(DEPTH):
        j = i // 2
        if i % 2 == 0:
            wall, wq, wkv = _ab_weights(ab_w_in[j], ab_w_q_up[j], ab_w_kv_up[j])
            qa, ka, va, qb, kb, vb = _ab_in_proj(
                x2, wall, row_vec(ab_q_norm[j]), wq, row_vec(ab_kv_norm[j]), wkv, ca, sa, cb, sb)
            out_a = _dilated_attention(qa, ka, va, batch, seq)
            out_b = _mla_attention(qb, kb, vb, batch, seq)
            w_out = ab_w_out[j].astype(BF16)
            acts, weights = [out_a, out_b], [w_out[:A_WIDTH], w_out[A_WIDTH:]]
        else:
            lambda_init = 0.8 - 0.6 * math.exp(-0.3 * i)
            q, k, v = _c_in_proj(x2, c_w_qkv[j].astype(BF16), ca, sa)
            subln = row_vec(c_subln[j])
            acts = [_diff_attention(c_lambda[j], subln, q, k, v, batch, seq, lambda_init)]
            weights = [c_w_out[j].astype(BF16)]
        x2 = _out_proj_ln(x2, acts, weights, row_vec(ln_mix_g[i]), row_vec(ln_mix_b[i]))
        wgu, cw, cbias, wd = _ffn_weights(
            ffn_w_gate[i], ffn_w_up[i], ffn_conv_w[i], ffn_conv_b[i], ffn_w_down[i])
        x2 = _ffn_block(x2, p[i].reshape(m, PLE_DIM), wgu, cw, cbias, wd, row_vec(ln_ffn_g[i]),
                        row_vec(ln_ffn_b[i]), ple_w_gate[i].astype(BF16),
                        ple_w_proj[i].astype(BF16), seq)
    return x2.reshape(batch, seq, d)
```

```python
import functools
import math

import numpy as np
import jax
import jax.numpy as jnp
from jax import lax
from jax.experimental import pallas as pl
from jax.experimental.pallas import tpu as pltpu

F32 = jnp.float32
BF16 = jnp.bfloat16

D_MODEL = 1024
DEPTH = 2
PLE_DIM = 256
ROPE_THETA = 500000.0
A_HEAD_DIM = 64
A_HEADS = 8
A_ROT = 16
A_PATTERNS = ((128, 1), (512, 4), (2048, 16))
A_SIDE = 64
B_HEADS = 8
B_Q_LORA = 384
B_KV_LORA = 256
B_NOPE = 64
B_ROPE = 32
B_V = 64
C_HEAD_DIM = 64
C_HEADS = 8
C_ROT = 16
D_FF = 2816
LN_EPS = 1e-5
RMS_EPS = 1e-6
NEG_INF = -1e30
ALPHA = (2 * DEPTH) ** 0.25
A_WIDTH = A_HEADS * A_HEAD_DIM
LOG2E = 1.4426950408889634

LANES = 128
VMEM_LIMIT = 56 * 1024 * 1024

PROJ_ROWS = 256
FFN_ROWS = 512
FFN_CHUNK = 256
HALO = 8
ATT_Q_ROWS = 256
ATT_UNROLL = 8
DIL_Q_ROWS = 128


def _params(n_axes):
    return pltpu.CompilerParams(
        dimension_semantics=("arbitrary",) * n_axes, vmem_limit_bytes=VMEM_LIMIT)


def _dot(a, b):
    return jnp.dot(a, b, preferred_element_type=F32)


def _dot_nt(a, b):
    return lax.dot_general(a, b, (((1,), (1,)), ((), ())), preferred_element_type=F32)


def _rope_group(x, cos_t, sin_t, half, take_upper):
    upper = pltpu.roll(x, LANES - half, axis=1)
    lower = pltpu.roll(x, half, axis=1)
    return x * cos_t + jnp.where(take_upper, upper, lower) * sin_t


def _rope_wide(x, cos_t, sin_t, half, take_upper):
    groups = [
        _rope_group(x[:, g * LANES:(g + 1) * LANES], cos_t, sin_t, half, take_upper)
        for g in range(x.shape[1] // LANES)
    ]
    return jnp.concatenate(groups, axis=1)


def _rms_norm(h, gain, eps):
    ms = jnp.mean(h * h, axis=-1, keepdims=True)
    return h * lax.rsqrt(ms + eps) * gain


def _layer_norm(z, gain, bias):
    mu = jnp.mean(z, axis=-1, keepdims=True)
    zc = z - mu
    var = jnp.mean(zc * zc, axis=-1, keepdims=True)
    return zc * lax.rsqrt(var + LN_EPS) * gain + bias


def _lane_iota():
    return lax.broadcasted_iota(jnp.int32, (1, LANES), 1)


def _softmax_pv(s, v_aug, c):
    m = jnp.max(s, axis=-1, keepdims=True)
    p = jnp.exp2((s - m) * c).astype(BF16)
    return _dot(p, v_aug), m


def _ab_in_kernel(x_ref, wall_ref, qn_ref, wq_ref, kvn_ref, wkv_ref, ca_ref, sa_ref, cb_ref,
                  sb_ref, qa_ref, ka_ref, va_ref, qb_ref, kb_ref, vb_ref):
    lane = _lane_iota()
    upper_a = (lane % A_HEAD_DIM) < (A_ROT // 2)
    upper_b = (lane >= B_NOPE) & (lane < B_NOPE + B_ROPE // 2)
    xb = x_ref[...].astype(BF16)
    h = _dot(xb, wall_ref[...])
    ca, sa = ca_ref[...], sa_ref[...]
    cb, sb = cb_ref[...], sb_ref[...]
    o1, o2, o3 = A_WIDTH, 2 * A_WIDTH, 3 * A_WIDTH
    o4 = o3 + B_Q_LORA
    o5 = o4 + B_KV_LORA
    qa_ref[...] = _rope_wide(h[:, :o1], ca, sa, A_ROT // 2, upper_a)
    ka_ref[...] = _rope_wide(h[:, o1:o2], ca, sa, A_ROT // 2, upper_a)
    va_ref[...] = h[:, o2:o3]
    cq = _rms_norm(h[:, o3:o4], qn_ref[...], RMS_EPS).astype(BF16)
    qb = _dot(cq, wq_ref[...])
    qb_ref[...] = _rope_wide(qb, cb, sb, B_ROPE // 2, upper_b).astype(BF16)
    ckv = _rms_norm(h[:, o4:o5], kvn_ref[...], RMS_EPS).astype(BF16)
    kv = _dot(ckv, wkv_ref[...])
    kpe = _rope_group(h[:, o5:o5 + LANES], cb, sb, B_ROPE // 2, upper_b)
    kslots = B_HEADS * LANES
    kb_ref[...] = (kv[:, :kslots] + jnp.concatenate([kpe] * B_HEADS, axis=1)).astype(BF16)
    vb_ref[...] = kv[:, kslots:].astype(BF16)


def _ab_in_proj(x2, wall, qn, wq, kvn, wkv, ca, sa, cb, sb):
    m = x2.shape[0]
    tm = PROJ_ROWS
    row = lambda w: pl.BlockSpec((tm, w), lambda i: (i, 0))
    full = lambda a: pl.BlockSpec(a.shape, lambda i: (0, 0))
    out_shape = (
        jax.ShapeDtypeStruct((m, A_WIDTH), F32),
        jax.ShapeDtypeStruct((m, A_WIDTH), F32),
        jax.ShapeDtypeStruct((m, A_WIDTH), F32),
        jax.ShapeDtypeStruct((m, B_HEADS * LANES), BF16),
        jax.ShapeDtypeStruct((m, B_HEADS * LANES), BF16),
        jax.ShapeDtypeStruct((m, B_HEADS * B_V), BF16),
    )
    return pl.pallas_call(
        _ab_in_kernel,
        grid=(m // tm,),
        in_specs=[row(D_MODEL), full(wall), full(qn), full(wq), full(kvn), full(wkv),
                  row(LANES), row(LANES), row(LANES), row(LANES)],
        out_specs=(row(A_WIDTH), row(A_WIDTH), row(A_WIDTH), row(B_HEADS * LANES),
                   row(B_HEADS * LANES), row(B_HEADS * B_V)),
        out_shape=out_shape,
        compiler_params=_params(1),
        name="ab_in_proj",
    )(x2, wall, qn, wq, kvn, wkv, ca, sa, cb, sb)


def _band_bias(tq, tk, offset):
    delta = np.arange(tk)[None, :] - np.arange(tq)[:, None] + offset
    bias = np.where(np.abs(delta) <= A_SIDE, 0.0, NEG_INF).astype(np.float32)
    return np.concatenate([bias, bias], axis=0)


def _window_start(q0, length, tk):
    return min(max(q0 - A_SIDE, 0), length - tk)


def _dilated_kernel(q_ref, k_ref, v_ref, bias_w_ref, bias_s_ref, o_ref, *stat_refs):
    seq = q_ref.shape[0]
    tq = DIL_Q_ROWS
    c = (A_HEAD_DIM ** -0.5) * LOG2E
    head0 = _lane_iota() < A_HEAD_DIM

    for pat, (window, dil) in enumerate(A_PATTERNS):
        assert window // (2 * dil) == A_SIDE
        num_ref, m_ref, l_ref = stat_refs[3 * pat:3 * pat + 3]
        length = seq // dil
        tk = min(tq + 2 * A_SIDE, length)
        ones = jnp.ones((tk, LANES), BF16)
        for res in range(dil):
            for q0 in range(0, length, tq):
                ws = _window_start(q0, length, tk)
                if dil == 1:
                    qrows, krows = pl.ds(q0, tq), pl.ds(ws, tk)
                else:
                    qrows = pl.ds(res + q0 * dil, tq, stride=dil)
                    krows = pl.ds(res + ws * dil, tk, stride=dil)
                if tk == length:
                    bias = bias_s_ref[...]
                else:
                    bias = bias_w_ref[(0, A_SIDE, 2 * A_SIDE).index(q0 - ws)]
                q = q_ref[qrows, :]
                ql = jnp.concatenate(
                    [jnp.where(head0, q, 0.0), jnp.where(head0, 0.0, q)], axis=0).astype(BF16)
                k = k_ref[krows, :].astype(BF16)
                v_aug = jnp.concatenate([v_ref[krows, :].astype(BF16), ones], axis=1)
                r, m = _softmax_pv(_dot_nt(ql, k) + bias, v_aug, c)
                num_ref[qrows, :] = jnp.where(head0, r[:tq, :LANES], r[tq:, :LANES])
                l_ref[qrows, :] = jnp.where(head0, r[:tq, LANES:], r[tq:, LANES:])
                m_ref[qrows, :] = jnp.where(head0, m[:tq], m[tq:])

    rows = 2 * tq
    n_pat = len(A_PATTERNS)
    for r0 in range(0, seq, rows):
        sl = pl.ds(r0, rows)
        ms = [stat_refs[3 * pat + 1][sl, :] for pat in range(n_pat)]
        m_all = functools.reduce(jnp.maximum, ms)
        scale = [jnp.exp2((mp - m_all) * c) for mp in ms]
        num = sum(stat_refs[3 * pat][sl, :] * scale[pat] for pat in range(n_pat))
        den = sum(stat_refs[3 * pat + 2][sl, :] * scale[pat] for pat in range(n_pat))
        o_ref[sl, :] = (num / den).astype(o_ref.dtype)


def _dilated_attention(qa, ka, va, batch, seq):
    tq = DIL_Q_ROWS
    q3, k3, v3 = (t.reshape(batch, seq, A_WIDTH) for t in (qa, ka, va))
    tkw = tq + 2 * A_SIDE
    bias_w = jnp.asarray(np.stack([_band_bias(tq, tkw, -off) for off in (0, A_SIDE, 2 * A_SIDE)]))
    bias_s = jnp.asarray(_band_bias(tq, tq, 0))
    spec = pl.BlockSpec((None, seq, LANES), lambda b, j: (b, 0, j))
    full = lambda a: pl.BlockSpec(a.shape, lambda b, j: (0,) * a.ndim)
    out = pl.pallas_call(
        _dilated_kernel,
        grid=(batch, A_WIDTH // LANES),
        in_specs=[spec, spec, spec, full(bias_w), full(bias_s)],
        out_specs=spec,
        out_shape=jax.ShapeDtypeStruct((batch, seq, A_WIDTH), BF16),
        scratch_shapes=[pltpu.VMEM((seq, LANES), F32)] * (3 * len(A_PATTERNS)),
        compiler_params=_params(2),
        name="dilated_attention",
    )(q3, k3, v3, bias_w, bias_s)
    return out.reshape(batch * seq, A_WIDTH)


def _mla_attn_kernel(q_ref, k_ref, v_ref, o_ref, vaug_ref):
    seq = q_ref.shape[0]
    tq = ATT_Q_ROWS
    c = ((B_NOPE + B_ROPE) ** -0.5) * LOG2E
    head0 = _lane_iota() < B_V
    vaug_ref[:, :LANES] = v_ref[...]
    vaug_ref[:, LANES:] = jnp.ones((seq, LANES), BF16)

    def tile(i, carry):
        rows = pl.ds(pl.multiple_of(i * tq, tq), tq)
        outs = []
        for h in range(2):
            q = q_ref[rows, h * LANES:(h + 1) * LANES]
            k = k_ref[:, h * LANES:(h + 1) * LANES]
            r, _ = _softmax_pv(_dot_nt(q, k), vaug_ref[...], c)
            outs.append(r[:, :LANES] / r[:, LANES:])
        o_ref[rows, :] = jnp.where(head0, outs[0], outs[1]).astype(o_ref.dtype)
        return carry

    lax.fori_loop(0, seq // tq, tile, 0, unroll=ATT_UNROLL)


def _mla_attention(qb, kb, vb, batch, seq):
    q3 = qb.reshape(batch, seq, B_HEADS * LANES)
    k3 = kb.reshape(batch, seq, B_HEADS * LANES)
    v3 = vb.reshape(batch, seq, B_HEADS * B_V)
    qk_spec = pl.BlockSpec((None, seq, 2 * LANES), lambda b, j: (b, 0, j))
    v_spec = pl.BlockSpec((None, seq, LANES), lambda b, j: (b, 0, j))
    out = pl.pallas_call(
        _mla_attn_kernel,
        grid=(batch, B_HEADS // 2),
        in_specs=[qk_spec, qk_spec, v_spec],
        out_specs=v_spec,
        out_shape=jax.ShapeDtypeStruct((batch, seq, B_HEADS * B_V), BF16),
        scratch_shapes=[pltpu.VMEM((seq, 2 * LANES), BF16)],
        compiler_params=_params(2),
        name="mla_attention",
    )(q3, k3, v3)
    return out.reshape(batch * seq, B_HEADS * B_V)


def _diff_attn_kernel(lam_ref, subln_ref, q_ref, k_ref, v_ref, o_ref, vaug_ref, *, lambda_init):
    seq = q_ref.shape[0]
    tq = ATT_Q_ROWS
    c = (C_HEAD_DIM ** -0.5) * LOG2E
    lp = lam_ref[...]
    t1 = jnp.sum(lp[0:1, :] * lp[1:2, :], axis=-1, keepdims=True)
    t2 = jnp.sum(lp[2:3, :] * lp[3:4, :], axis=-1, keepdims=True)
    lam = jnp.exp(t1) - jnp.exp(t2) + lambda_init
    comp0 = _lane_iota() < C_HEAD_DIM
    gain = subln_ref[...] * (1.0 - lambda_init)
    vaug_ref[:, :LANES] = v_ref[...]
    vaug_ref[:, LANES:] = jnp.ones((seq, LANES), BF16)

    def tile(i, carry):
        rows = pl.ds(pl.multiple_of(i * tq, tq), tq)
        q = q_ref[rows, :]
        zero = jnp.zeros_like(q)
        ql = jnp.concatenate([jnp.where(comp0, q, zero), jnp.where(comp0, zero, q)], axis=0)
        r, _ = _softmax_pv(_dot_nt(ql, k_ref[...]), vaug_ref[...], c)
        o = r[:tq, :LANES] / r[:tq, LANES:] - lam * (r[tq:, :LANES] / r[tq:, LANES:])
        o_ref[rows, :] = _rms_norm(o, gain, LN_EPS).astype(o_ref.dtype)
        return carry

    lax.fori_loop(0, seq // tq, tile, 0, unroll=ATT_UNROLL)


def _diff_attention(lam_params, subln, q, k, v, batch, seq, lambda_init):
    width = C_HEADS * 2 * C_HEAD_DIM
    q3, k3, v3 = (t.reshape(batch, seq, width) for t in (q, k, v))
    full = lambda a: pl.BlockSpec(a.shape, lambda b, h: (0, 0))
    spec = pl.BlockSpec((None, seq, LANES), lambda b, h: (b, 0, h))
    out = pl.pallas_call(
        functools.partial(_diff_attn_kernel, lambda_init=lambda_init),
        grid=(batch, C_HEADS),
        in_specs=[full(lam_params), full(subln), spec, spec, spec],
        out_specs=spec,
        out_shape=jax.ShapeDtypeStruct((batch, seq, width), BF16),
        scratch_shapes=[pltpu.VMEM((seq, 2 * LANES), BF16)],
        compiler_params=_params(2),
        name="diff_attention",
    )(lam_params, subln, q3, k3, v3)
    return out.reshape(batch * seq, width)


def _c_in_kernel(x_ref, w_ref, ca_ref, sa_ref, q_ref, k_ref, v_ref):
    upper = (_lane_iota() % C_HEAD_DIM) < (C_ROT // 2)
    width = q_ref.shape[1]
    h = _dot(x_ref[...].astype(BF16), w_ref[...])
    ca, sa = ca_ref[...], sa_ref[...]
    q_ref[...] = _rope_wide(h[:, :width], ca, sa, C_ROT // 2, upper).astype(BF16)
    k_ref[...] = _rope_wide(h[:, width:2 * width], ca, sa, C_ROT // 2, upper).astype(BF16)
    v_ref[...] = h[:, 2 * width:].astype(BF16)


def _c_in_proj(x2, w, ca, sa):
    m = x2.shape[0]
    tm = PROJ_ROWS
    width = w.shape[1] // 3
    row = lambda n: pl.BlockSpec((tm, n), lambda i: (i, 0))
    out = jax.ShapeDtypeStruct((m, width), BF16)
    return pl.pallas_call(
        _c_in_kernel,
        grid=(m // tm,),
        in_specs=[row(D_MODEL), pl.BlockSpec(w.shape, lambda i: (0, 0)), row(LANES), row(LANES)],
        out_specs=(row(width), row(width), row(width)),
        out_shape=(out, out, out),
        compiler_params=_params(1),
        name="c_in_proj",
    )(x2, w, ca, sa)


def _out_proj_ln_kernel(*refs, n_in):
    x_ref = refs[0]
    a_refs = refs[1:1 + n_in]
    w_refs = refs[1 + n_in:1 + 2 * n_in]
    g_ref, b_ref, o_ref = refs[1 + 2 * n_in:]
    y = _dot(a_refs[0][...], w_refs[0][...])
    for a_ref, w_ref in zip(a_refs[1:], w_refs[1:]):
        y = y + _dot(a_ref[...], w_ref[...])
    o_ref[...] = _layer_norm(ALPHA * x_ref[...] + y, g_ref[...], b_ref[...])


def _out_proj_ln(x2, acts, weights, gain, bias):
    m = x2.shape[0]
    tm = PROJ_ROWS
    row = lambda n: pl.BlockSpec((tm, n), lambda i: (i, 0))
    full = lambda a: pl.BlockSpec(a.shape, lambda i: (0, 0))
    return pl.pallas_call(
        functools.partial(_out_proj_ln_kernel, n_in=len(acts)),
        grid=(m // tm,),
        in_specs=([row(D_MODEL)] + [row(a.shape[1]) for a in acts] + [full(w) for w in weights]
                  + [full(gain), full(bias)]),
        out_specs=row(D_MODEL),
        out_shape=jax.ShapeDtypeStruct((m, D_MODEL), F32),
        compiler_params=_params(1),
        name="out_proj_ln",
    )(x2, *acts, *weights, gain, bias)


def _gelu_tanh(c):
    return 0.5 * c * (1.0 + jnp.tanh(math.sqrt(2.0 / math.pi) * (c + 0.044715 * (c * c * c))))


def _ffn_kernel(x_ref, prev_ref, next_ref, p_ref, wgu_ref, cw_ref, cb_ref, wd_ref, g_ref, b_ref,
                wpg_ref, wpp_ref, o_ref, g_scr, *, tiles_per_seq):
    tm = x_ref.shape[0]
    n_chunks = wgu_ref.shape[0]
    tn = FFN_CHUNK
    i = pl.program_id(0)
    pos = lax.rem(i, tiles_per_seq)
    x = x_ref[...]
    prev = jnp.where(pos == 0, 0.0, prev_ref[...])
    nxt = jnp.where(pos == tiles_per_seq - 1, 0.0, next_ref[...])
    xb = jnp.concatenate([prev, x, nxt], axis=0).astype(BF16)
    rows = tm + 2 * HALO
    for n in range(n_chunks):
        au = _dot(xb, wgu_ref[n])
        a = au[:, :tn]
        u = au[HALO:HALO + tm, tn:]
        cw = cw_ref[n]
        a_prev = pltpu.roll(a, 1, axis=0)[HALO:HALO + tm]
        a_next = pltpu.roll(a, rows - 1, axis=0)[HALO:HALO + tm]
        conv = (cb_ref[n] + cw[0:1, :] * a_prev + cw[1:2, :] * a[HALO:HALO + tm]
                + cw[2:3, :] * a_next)
        g_scr[:, n * tn:(n + 1) * tn] = (_gelu_tanh(conv) * u).astype(BF16)
    f = _dot(g_scr[...], wd_ref[...])
    x2 = _layer_norm(ALPHA * x + f, g_ref[...], b_ref[...])
    gate = jax.nn.sigmoid(_dot(x2.astype(BF16), wpg_ref[...]))
    proj = _dot(p_ref[...].astype(BF16), wpp_ref[...])
    o_ref[...] = x2 + gate * proj


def _ffn_block(x2, p2, wgu, cw, cb, wd, gain, bias, wpg, wpp, seq):
    m = x2.shape[0]
    tm = FFN_ROWS
    halo_blocks = tm // HALO
    row = lambda n: pl.BlockSpec((tm, n), lambda i: (i, 0))

    def resident(a):
        nd = a.ndim
        return pl.BlockSpec(a.shape, lambda i: (0,) * nd, pipeline_mode=pl.Buffered(1))

    prev_spec = pl.BlockSpec((HALO, D_MODEL), lambda i: (jnp.maximum(i * halo_blocks - 1, 0), 0))
    next_spec = pl.BlockSpec(
        (HALO, D_MODEL), lambda i: (jnp.minimum((i + 1) * halo_blocks, m // HALO - 1), 0))
    return pl.pallas_call(
        functools.partial(_ffn_kernel, tiles_per_seq=seq // tm),
        grid=(m // tm,),
        in_specs=[row(D_MODEL), prev_spec, next_spec, row(PLE_DIM), resident(wgu), resident(cw),
                  resident(cb), resident(wd), resident(gain), resident(bias), resident(wpg),
                  resident(wpp)],
        out_specs=row(D_MODEL),
        out_shape=jax.ShapeDtypeStruct((m, D_MODEL), F32),
        scratch_shapes=[pltpu.VMEM((tm, D_FF), BF16)],
        compiler_params=_params(1),
        name="ffn_block",
    )(x2, x2, x2, p2, wgu, cw, cb, wd, gain, bias, wpg, wpp)


def _lane_table(pos, rot, rot_start, period):
    inv_freq = 1.0 / (ROPE_THETA ** (jnp.arange(0, rot, 2, dtype=F32) / rot))
    rel = np.arange(LANES) % period - rot_start
    rotary = (rel >= 0) & (rel < rot)
    sign = np.where(rel < rot // 2, -1.0, 1.0).astype(np.float32)
    ang = pos * inv_freq[np.where(rotary, rel % (rot // 2), 0)][None, :]
    cos_t = jnp.where(rotary[None, :], jnp.cos(ang), 1.0)
    sin_t = jnp.where(rotary[None, :], jnp.sin(ang) * sign[None, :], 0.0)
    return cos_t, sin_t


def _lane_tables(positions):
    pos = positions.reshape(-1, 1).astype(F32)
    ca, sa = _lane_table(pos, A_ROT, 0, A_HEAD_DIM)
    cb, sb = _lane_table(pos, B_ROPE, B_NOPE, LANES)
    return ca, sa, cb, sb


def _ab_weights(w_in, w_q_up, w_kv_up):
    o3 = 3 * A_WIDTH
    o5 = o3 + B_Q_LORA + B_KV_LORA
    k_pe = jnp.pad(w_in[:, o5:], ((0, 0), (B_NOPE, LANES - B_NOPE - B_ROPE)))
    wall = jnp.concatenate([w_in[:, :o5], k_pe], axis=1).astype(BF16)
    per_head = B_NOPE + B_ROPE
    wq = jnp.pad(w_q_up.reshape(B_Q_LORA, B_HEADS, per_head),
                 ((0, 0), (0, 0), (0, LANES - per_head))).reshape(B_Q_LORA, B_HEADS * LANES)
    kv = w_kv_up.reshape(B_KV_LORA, B_HEADS, B_NOPE + B_V)
    wk = jnp.pad(kv[:, :, :B_NOPE], ((0, 0), (0, 0), (0, LANES - B_NOPE)))
    wkv = jnp.concatenate([wk.reshape(B_KV_LORA, B_HEADS * LANES),
                           kv[:, :, B_NOPE:].reshape(B_KV_LORA, B_HEADS * B_V)], axis=1)
    return wall, wq.astype(BF16), wkv.astype(BF16)


def _ffn_weights(w_gate, w_up, conv_w, conv_b, w_down):
    n_chunks = D_FF // FFN_CHUNK
    chunks = lambda w: w.reshape(w.shape[0], n_chunks, FFN_CHUNK).transpose(1, 0, 2)
    wgu = jnp.concatenate([chunks(w_gate), chunks(w_up)], axis=2).astype(BF16)
    cw = chunks(conv_w)
    cb = conv_b.reshape(n_chunks, 1, FFN_CHUNK)
    return wgu, cw, cb, w_down.astype(BF16)


def kernel(x, p, positions, ab_w_in, ab_q_norm, ab_w_q_up, ab_kv_norm, ab_w_kv_up, ab_w_out,
           c_w_qkv, c_lambda, c_subln, c_w_out, ln_mix_g, ln_mix_b, ffn_w_gate, ffn_w_up,
           ffn_conv_w, ffn_conv_b, ffn_w_down, ln_ffn_g, ln_ffn_b, ple_w_gate, ple_w_proj):
    batch, seq, d = x.shape
    m = batch * seq
    ca, sa, cb, sb = _lane_tables(positions)
    x2 = x.reshape(m, d)
    row_vec = lambda v: v.reshape(1, -1)
    for i in range(DEPTH):
        j = i // 2
        if i % 2 == 0:
            wall, wq, wkv = _ab_weights(ab_w_in[j], ab_w_q_up[j], ab_w_kv_up[j])
            qa, ka, va, qb, kb, vb = _ab_in_proj(
                x2, wall, row_vec(ab_q_norm[j]), wq, row_vec(ab_kv_norm[j]), wkv, ca, sa, cb, sb)
            out_a = _dilated_attention(qa, ka, va, batch, seq)
            out_b = _mla_attention(qb, kb, vb, batch, seq)
            w_out = ab_w_out[j].astype(BF16)
            acts, weights = [out_a, out_b], [w_out[:A_WIDTH], w_out[A_WIDTH:]]
        else:
            lambda_init = 0.8 - 0.6 * math.exp(-0.3 * i)
            q, k, v = _c_in_proj(x2, c_w_qkv[j].astype(BF16), ca, sa)
            subln = row_vec(c_subln[j])
            acts = [_diff_attention(c_lambda[j], subln, q, k, v, batch, seq, lambda_init)]
            weights = [c_w_out[j].astype(BF16)]
        x2 = _out_proj_ln(x2, acts, weights, row_vec(ln_mix_g[i]), row_vec(ln_mix_b[i]))
        wgu, cw, cbias, wd = _ffn_weights(
            ffn_w_gate[i], ffn_w_up[i], ffn_conv_w[i], ffn_conv_b[i], ffn_w_down[i])
        x2 = _ffn_block(x2, p[i].reshape(m, PLE_DIM), wgu, cw, cbias, wd, row_vec(ln_ffn_g[i]),
                        row_vec(ln_ffn_b[i]), ple_w_gate[i].astype(BF16),
                        ple_w_proj[i].astype(BF16), seq)
    return x2.reshape(batch, seq, d)
```

```python
import functools
import math

import numpy as np
import jax
import jax.numpy as jnp
from jax import lax
from jax.experimental import pallas as pl
from jax.experimental.pallas import tpu as pltpu

F32 = jnp.float32
BF16 = jnp.bfloat16

D_MODEL = 1024
DEPTH = 2
PLE_DIM = 256
ROPE_THETA = 500000.0
A_HEAD_DIM = 64
A_HEADS = 8
A_ROT = 16
A_PATTERNS = ((128, 1), (512, 4), (2048, 16))
A_SIDE = 64
B_HEADS = 8
B_Q_LORA = 384
B_KV_LORA = 256
B_NOPE = 64
B_ROPE = 32
B_V = 64
C_HEAD_DIM = 64
C_HEADS = 8
C_ROT = 16
D_FF = 2816
LN_EPS = 1e-5
RMS_EPS = 1e-6
NEG_INF = -1e30
ALPHA = (2 * DEPTH) ** 0.25
A_WIDTH = A_HEADS * A_HEAD_DIM
LOG2E = 1.4426950408889634

LANES = 128
VMEM_LIMIT = 56 * 1024 * 1024

PROJ_ROWS = 512
FFN_ROWS = 1024
FFN_CHUNK = 256
HALO = 8
ATT_Q_ROWS = 256
ATT_UNROLL = 8
DIL_Q_ROWS = 128


def _params(n_axes):
    return pltpu.CompilerParams(
        dimension_semantics=("arbitrary",) * n_axes, vmem_limit_bytes=VMEM_LIMIT)


def _dot(a, b):
    return jnp.dot(a, b, preferred_element_type=F32)


def _dot_nt(a, b):
    return lax.dot_general(a, b, (((1,), (1,)), ((), ())), preferred_element_type=F32)


def _rope_group(x, cos_t, sin_t, half, take_upper):
    upper = pltpu.roll(x, LANES - half, axis=1)
    lower = pltpu.roll(x, half, axis=1)
    return x * cos_t + jnp.where(take_upper, upper, lower) * sin_t


def _rope_wide(x, cos_t, sin_t, half, take_upper):
    groups = [
        _rope_group(x[:, g * LANES:(g + 1) * LANES], cos_t, sin_t, half, take_upper)
        for g in range(x.shape[1] // LANES)
    ]
    return jnp.concatenate(groups, axis=1)


def _rope_lane_tables(cos_c, sin_c, rot, rot_start, period, base):
    rel = _lane_iota() % period - rot_start
    rotary = (rel >= 0) & (rel < rot)
    first = rel < rot // 2
    freq = jnp.where(first, rel, rel - rot // 2)
    idx = jnp.broadcast_to(jnp.where(rotary, base + freq, 0), cos_c.shape)
    cos_t = jnp.where(rotary, jnp.take_along_axis(cos_c, idx, axis=1), 1.0)
    sin_g = jnp.take_along_axis(sin_c, idx, axis=1)
    sin_t = jnp.where(rotary, jnp.where(first, -sin_g, sin_g), 0.0)
    return cos_t, sin_t


def _rms_norm(h, gain, eps):
    ms = jnp.mean(h * h, axis=-1, keepdims=True)
    return h * lax.rsqrt(ms + eps) * gain


def _layer_norm(z, gain, bias):
    mu = jnp.mean(z, axis=-1, keepdims=True)
    zc = z - mu
    var = jnp.mean(zc * zc, axis=-1, keepdims=True)
    return zc * lax.rsqrt(var + LN_EPS) * gain + bias


def _lane_iota():
    return lax.broadcasted_iota(jnp.int32, (1, LANES), 1)


def _softmax_pv(s, v_aug, c):
    m = jnp.max(s, axis=-1, keepdims=True)
    p = jnp.exp2((s - m) * c).astype(BF16)
    return _dot(p, v_aug), m


def _ab_in_kernel(x_ref, wall_ref, qn_ref, wq_ref, kvn_ref, wkv_ref, cos_ref, sin_ref,
                  qa_ref, ka_ref, va_ref, qb_ref, kb_ref, vb_ref):
    lane = _lane_iota()
    upper_a = (lane % A_HEAD_DIM) < (A_ROT // 2)
    upper_b = (lane >= B_NOPE) & (lane < B_NOPE + B_ROPE // 2)
    xb = x_ref[...].astype(BF16)
    h = _dot(xb, wall_ref[...])
    cos_c, sin_c = cos_ref[...], sin_ref[...]
    ca, sa = _rope_lane_tables(cos_c, sin_c, A_ROT, 0, A_HEAD_DIM, 0)
    cb, sb = _rope_lane_tables(cos_c, sin_c, B_ROPE, B_NOPE, LANES, A_ROT // 2)
    o1, o2, o3 = A_WIDTH, 2 * A_WIDTH, 3 * A_WIDTH
    o4 = o3 + B_Q_LORA
    o5 = o4 + B_KV_LORA
    qa_ref[...] = _rope_wide(h[:, :o1], ca, sa, A_ROT // 2, upper_a)
    ka_ref[...] = _rope_wide(h[:, o1:o2], ca, sa, A_ROT // 2, upper_a)
    va_ref[...] = h[:, o2:o3]
    cq = _rms_norm(h[:, o3:o4], qn_ref[...], RMS_EPS).astype(BF16)
    qb = _dot(cq, wq_ref[...])
    qb_ref[...] = _rope_wide(qb, cb, sb, B_ROPE // 2, upper_b).astype(BF16)
    ckv = _rms_norm(h[:, o4:o5], kvn_ref[...], RMS_EPS).astype(BF16)
    kv = _dot(ckv, wkv_ref[...])
    kpe = _rope_group(h[:, o5:o5 + LANES], cb, sb, B_ROPE // 2, upper_b)
    kslots = B_HEADS * LANES
    kb_ref[...] = (kv[:, :kslots] + jnp.concatenate([kpe] * B_HEADS, axis=1)).astype(BF16)
    vb_ref[...] = kv[:, kslots:].astype(BF16)


def _ab_in_proj(x2, wall, qn, wq, kvn, wkv, cos_c, sin_c):
    m = x2.shape[0]
    tm = PROJ_ROWS
    row = lambda w: pl.BlockSpec((tm, w), lambda i: (i, 0))
    full = lambda a: pl.BlockSpec(a.shape, lambda i: (0, 0))
    out_shape = (
        jax.ShapeDtypeStruct((m, A_WIDTH), F32),
        jax.ShapeDtypeStruct((m, A_WIDTH), F32),
        jax.ShapeDtypeStruct((m, A_WIDTH), F32),
        jax.ShapeDtypeStruct((m, B_HEADS * LANES), BF16),
        jax.ShapeDtypeStruct((m, B_HEADS * LANES), BF16),
        jax.ShapeDtypeStruct((m, B_HEADS * B_V), BF16),
    )
    return pl.pallas_call(
        _ab_in_kernel,
        grid=(m // tm,),
        in_specs=[row(D_MODEL), full(wall), full(qn), full(wq), full(kvn), full(wkv),
                  row(LANES), row(LANES)],
        out_specs=(row(A_WIDTH), row(A_WIDTH), row(A_WIDTH), row(B_HEADS * LANES),
                   row(B_HEADS * LANES), row(B_HEADS * B_V)),
        out_shape=out_shape,
        compiler_params=_params(1),
        name="ab_in_proj",
    )(x2, wall, qn, wq, kvn, wkv, cos_c, sin_c)


def _band_bias(tq, tk, offset):
    delta = np.arange(tk)[None, :] - np.arange(tq)[:, None] + offset
    bias = np.where(np.abs(delta) <= A_SIDE, 0.0, NEG_INF).astype(np.float32)
    return np.concatenate([bias, bias], axis=0)


def _window_start(q0, length, tk):
    return min(max(q0 - A_SIDE, 0), length - tk)


def _dilated_kernel(q_ref, k_ref, v_ref, bias_w_ref, bias_s_ref, o_ref, *stat_refs):
    seq = q_ref.shape[0]
    tq = DIL_Q_ROWS
    c = (A_HEAD_DIM ** -0.5) * LOG2E
    head0 = _lane_iota() < A_HEAD_DIM

    for pat, (window, dil) in enumerate(A_PATTERNS):
        assert window // (2 * dil) == A_SIDE
        num_ref, m_ref, l_ref = stat_refs[3 * pat:3 * pat + 3]
        length = seq // dil
        tk = min(tq + 2 * A_SIDE, length)
        ones = jnp.ones((tk, LANES), BF16)
        for res in range(dil):
            for q0 in range(0, length, tq):
                ws = _window_start(q0, length, tk)
                if dil == 1:
                    qrows, krows = pl.ds(q0, tq), pl.ds(ws, tk)
                else:
                    qrows = pl.ds(res + q0 * dil, tq, stride=dil)
                    krows = pl.ds(res + ws * dil, tk, stride=dil)
                if tk == length:
                    bias = bias_s_ref[...]
                else:
                    bias = bias_w_ref[(0, A_SIDE, 2 * A_SIDE).index(q0 - ws)]
                q = q_ref[qrows, :]
                ql = jnp.concatenate(
                    [jnp.where(head0, q, 0.0), jnp.where(head0, 0.0, q)], axis=0).astype(BF16)
                k = k_ref[krows, :].astype(BF16)
                v_aug = jnp.concatenate([v_ref[krows, :].astype(BF16), ones], axis=1)
                r, m = _softmax_pv(_dot_nt(ql, k) + bias, v_aug, c)
                num_ref[qrows, :] = jnp.where(head0, r[:tq, :LANES], r[tq:, :LANES])
                l_ref[qrows, :] = jnp.where(head0, r[:tq, LANES:], r[tq:, LANES:])
                m_ref[qrows, :] = jnp.where(head0, m[:tq], m[tq:])

    rows = 2 * tq
    n_pat = len(A_PATTERNS)
    for r0 in range(0, seq, rows):
        sl = pl.ds(r0, rows)
        ms = [stat_refs[3 * pat + 1][sl, :] for pat in range(n_pat)]
        m_all = functools.reduce(jnp.maximum, ms)
        scale = [jnp.exp2((mp - m_all) * c) for mp in ms]
        num = sum(stat_refs[3 * pat][sl, :] * scale[pat] for pat in range(n_pat))
        den = sum(stat_refs[3 * pat + 2][sl, :] * scale[pat] for pat in range(n_pat))
        o_ref[sl, :] = (num / den).astype(o_ref.dtype)


def _dilated_attention(qa, ka, va, batch, seq):
    tq = DIL_Q_ROWS
    q3, k3, v3 = (t.reshape(batch, seq, A_WIDTH) for t in (qa, ka, va))
    tkw = tq + 2 * A_SIDE
    bias_w = jnp.asarray(np.stack([_band_bias(tq, tkw, -off) for off in (0, A_SIDE, 2 * A_SIDE)]))
    bias_s = jnp.asarray(_band_bias(tq, tq, 0))
    spec = pl.BlockSpec((None, seq, LANES), lambda b, j: (b, 0, j))
    full = lambda a: pl.BlockSpec(a.shape, lambda b, j: (0,) * a.ndim)
    out = pl.pallas_call(
        _dilated_kernel,
        grid=(batch, A_WIDTH // LANES),
        in_specs=[spec, spec, spec, full(bias_w), full(bias_s)],
        out_specs=spec,
        out_shape=jax.ShapeDtypeStruct((batch, seq, A_WIDTH), BF16),
        scratch_shapes=[pltpu.VMEM((seq, LANES), F32)] * (3 * len(A_PATTERNS)),
        compiler_params=_params(2),
        name="dilated_attention",
    )(q3, k3, v3, bias_w, bias_s)
    return out.reshape(batch * seq, A_WIDTH)


def _mla_attn_kernel(q_ref, k_ref, v_ref, o_ref, vaug_ref):
    seq = q_ref.shape[0]
    tq = ATT_Q_ROWS
    c = ((B_NOPE + B_ROPE) ** -0.5) * LOG2E
    head0 = _lane_iota() < B_V
    vaug_ref[:, :LANES] = v_ref[...]
    vaug_ref[:, LANES:] = jnp.ones((seq, LANES), BF16)

    def tile(i, carry):
        rows = pl.ds(pl.multiple_of(i * tq, tq), tq)
        outs = []
        for h in range(2):
            q = q_ref[rows, h * LANES:(h + 1) * LANES]
            k = k_ref[:, h * LANES:(h + 1) * LANES]
            r, _ = _softmax_pv(_dot_nt(q, k), vaug_ref[...], c)
            outs.append(r[:, :LANES] / r[:, LANES:])
        o_ref[rows, :] = jnp.where(head0, outs[0], outs[1]).astype(o_ref.dtype)
        return carry

    lax.fori_loop(0, seq // tq, tile, 0, unroll=ATT_UNROLL)


def _mla_attention(qb, kb, vb, batch, seq):
    q3 = qb.reshape(batch, seq, B_HEADS * LANES)
    k3 = kb.reshape(batch, seq, B_HEADS * LANES)
    v3 = vb.reshape(batch, seq, B_HEADS * B_V)
    qk_spec = pl.BlockSpec((None, seq, 2 * LANES), lambda b, j: (b, 0, j))
    v_spec = pl.BlockSpec((None, seq, LANES), lambda b, j: (b, 0, j))
    out = pl.pallas_call(
        _mla_attn_kernel,
        grid=(batch, B_HEADS // 2),
        in_specs=[qk_spec, qk_spec, v_spec],
        out_specs=v_spec,
        out_shape=jax.ShapeDtypeStruct((batch, seq, B_HEADS * B_V), BF16),
        scratch_shapes=[pltpu.VMEM((seq, 2 * LANES), BF16)],
        compiler_params=_params(2),
        name="mla_attention",
    )(q3, k3, v3)
    return out.reshape(batch * seq, B_HEADS * B_V)


def _diff_attn_kernel(lam_ref, subln_ref, q_ref, k_ref, v_ref, o_ref, vaug_ref, *, lambda_init):
    seq = q_ref.shape[0]
    tq = ATT_Q_ROWS
    c = (C_HEAD_DIM ** -0.5) * LOG2E
    lp = lam_ref[...]
    t1 = jnp.sum(lp[0:1, :] * lp[1:2, :], axis=-1, keepdims=True)
    t2 = jnp.sum(lp[2:3, :] * lp[3:4, :], axis=-1, keepdims=True)
    lam = jnp.exp(t1) - jnp.exp(t2) + lambda_init
    comp0 = _lane_iota() < C_HEAD_DIM
    gain = subln_ref[...] * (1.0 - lambda_init)
    vaug_ref[:, :LANES] = v_ref[...]
    vaug_ref[:, LANES:] = jnp.ones((seq, LANES), BF16)

    def tile(i, carry):
        rows = pl.ds(pl.multiple_of(i * tq, tq), tq)
        q = q_ref[rows, :]
        zero = jnp.zeros_like(q)
        ql = jnp.concatenate([jnp.where(comp0, q, zero), jnp.where(comp0, zero, q)], axis=0)
        r, _ = _softmax_pv(_dot_nt(ql, k_ref[...]), vaug_ref[...], c)
        o = r[:tq, :LANES] / r[:tq, LANES:] - lam * (r[tq:, :LANES] / r[tq:, LANES:])
        o_ref[rows, :] = _rms_norm(o, gain, LN_EPS).astype(o_ref.dtype)
        return carry

    lax.fori_loop(0, seq // tq, tile, 0, unroll=ATT_UNROLL)


def _diff_attention(lam_params, subln, q, k, v, batch, seq, lambda_init):
    width = C_HEADS * 2 * C_HEAD_DIM
    q3, k3, v3 = (t.reshape(batch, seq, width) for t in (q, k, v))
    full = lambda a: pl.BlockSpec(a.shape, lambda b, h: (0, 0))
    spec = pl.BlockSpec((None, seq, LANES), lambda b, h: (b, 0, h))
    out = pl.pallas_call(
        functools.partial(_diff_attn_kernel, lambda_init=lambda_init),
        grid=(batch, C_HEADS),
        in_specs=[full(lam_params), full(subln), spec, spec, spec],
        out_specs=spec,
        out_shape=jax.ShapeDtypeStruct((batch, seq, width), BF16),
        scratch_shapes=[pltpu.VMEM((seq, 2 * LANES), BF16)],
        compiler_params=_params(2),
        name="diff_attention",
    )(lam_params, subln, q3, k3, v3)
    return out.reshape(batch * seq, width)


def _c_in_kernel(x_ref, w_ref, cos_ref, sin_ref, q_ref, k_ref, v_ref):
    upper = (_lane_iota() % C_HEAD_DIM) < (C_ROT // 2)
    width = q_ref.shape[1]
    h = _dot(x_ref[...].astype(BF16), w_ref[...])
    ca, sa = _rope_lane_tables(cos_ref[...], sin_ref[...], C_ROT, 0, C_HEAD_DIM, 0)
    q_ref[...] = _rope_wide(h[:, :width], ca, sa, C_ROT // 2, upper).astype(BF16)
    k_ref[...] = _rope_wide(h[:, width:2 * width], ca, sa, C_ROT // 2, upper).astype(BF16)
    v_ref[...] = h[:, 2 * width:].astype(BF16)


def _c_in_proj(x2, w, cos_c, sin_c):
    m = x2.shape[0]
    tm = PROJ_ROWS
    width = w.shape[1] // 3
    row = lambda n: pl.BlockSpec((tm, n), lambda i: (i, 0))
    out = jax.ShapeDtypeStruct((m, width), BF16)
    return pl.pallas_call(
        _c_in_kernel,
        grid=(m // tm,),
        in_specs=[row(D_MODEL), pl.BlockSpec(w.shape, lambda i: (0, 0)), row(LANES), row(LANES)],
        out_specs=(row(width), row(width), row(width)),
        out_shape=(out, out, out),
        compiler_params=_params(1),
        name="c_in_proj",
    )(x2, w, cos_c, sin_c)


def _out_proj_ln_kernel(*refs, n_in):
    x_ref = refs[0]
    a_refs = refs[1:1 + n_in]
    w_refs = refs[1 + n_in:1 + 2 * n_in]
    g_ref, b_ref, o_ref = refs[1 + 2 * n_in:]
    y = _dot(a_refs[0][...], w_refs[0][...])
    for a_ref, w_ref in zip(a_refs[1:], w_refs[1:]):
        y = y + _dot(a_ref[...], w_ref[...])
    o_ref[...] = _layer_norm(ALPHA * x_ref[...] + y, g_ref[...], b_ref[...])


def _out_proj_ln(x2, acts, weights, gain, bias):
    m = x2.shape[0]
    tm = PROJ_ROWS
    row = lambda n: pl.BlockSpec((tm, n), lambda i: (i, 0))
    full = lambda a: pl.BlockSpec(a.shape, lambda i: (0, 0))
    return pl.pallas_call(
        functools.partial(_out_proj_ln_kernel, n_in=len(acts)),
        grid=(m // tm,),
        in_specs=([row(D_MODEL)] + [row(a.shape[1]) for a in acts] + [full(w) for w in weights]
                  + [full(gain), full(bias)]),
        out_specs=row(D_MODEL),
        out_shape=jax.ShapeDtypeStruct((m, D_MODEL), F32),
        compiler_params=_params(1),
        name="out_proj_ln",
    )(x2, *acts, *weights, gain, bias)


def _gelu_tanh(c):
    return 0.5 * c * (1.0 + jnp.tanh(math.sqrt(2.0 / math.pi) * (c + 0.044715 * (c * c * c))))


def _ffn_kernel(x_ref, prev_ref, next_ref, p_ref, wg_ref, wu_ref, cw_ref, cb_ref, wd_ref, g_ref,
                b_ref, wpg_ref, wpp_ref, o_ref, g_scr, *, tiles_per_seq):
    tm = x_ref.shape[0]
    tn = FFN_CHUNK
    n_chunks = wg_ref.shape[1] // tn
    i = pl.program_id(0)
    pos = lax.rem(i, tiles_per_seq)
    x = x_ref[...]
    prev = jnp.where(pos == 0, 0.0, prev_ref[...])
    nxt = jnp.where(pos == tiles_per_seq - 1, 0.0, next_ref[...])
    xh = jnp.concatenate([prev, x, nxt], axis=0).astype(BF16)
    xb = xh[HALO:HALO + tm]
    rows = tm + 2 * HALO
    for n in range(n_chunks):
        cols = slice(n * tn, (n + 1) * tn)
        a = _dot(xh, wg_ref[:, cols])
        u = _dot(xb, wu_ref[:, cols])
        cw = cw_ref[:, cols]
        a_prev = pltpu.roll(a, 1, axis=0)[HALO:HALO + tm]
        a_next = pltpu.roll(a, rows - 1, axis=0)[HALO:HALO + tm]
        conv = (cb_ref[:, cols] + cw[0:1, :] * a_prev + cw[1:2, :] * a[HALO:HALO + tm]
                + cw[2:3, :] * a_next)
        g_scr[:, cols] = (_gelu_tanh(conv) * u).astype(BF16)
    f = _dot(g_scr[...], wd_ref[...])
    x2 = _layer_norm(ALPHA * x + f, g_ref[...], b_ref[...])
    gate = jax.nn.sigmoid(_dot(x2.astype(BF16), wpg_ref[...]))
    proj = _dot(p_ref[...].astype(BF16), wpp_ref[...])
    o_ref[...] = x2 + gate * proj


def _ffn_block(x2, p2, wg, wu, cw, cb, wd, gain, bias, wpg, wpp, seq):
    m = x2.shape[0]
    tm = FFN_ROWS
    halo_blocks = tm // HALO
    row = lambda n: pl.BlockSpec((tm, n), lambda i: (i, 0))

    def resident(a):
        nd = a.ndim
        return pl.BlockSpec(a.shape, lambda i: (0,) * nd, pipeline_mode=pl.Buffered(1))

    prev_spec = pl.BlockSpec((HALO, D_MODEL), lambda i: (jnp.maximum(i * halo_blocks - 1, 0), 0))
    next_spec = pl.BlockSpec(
        (HALO, D_MODEL), lambda i: (jnp.minimum((i + 1) * halo_blocks, m // HALO - 1), 0))
    return pl.pallas_call(
        functools.partial(_ffn_kernel, tiles_per_seq=seq // tm),
        grid=(m // tm,),
        in_specs=[row(D_MODEL), prev_spec, next_spec, row(PLE_DIM), resident(wg), resident(wu),
                  resident(cw), resident(cb), resident(wd), resident(gain), resident(bias),
                  resident(wpg), resident(wpp)],
        out_specs=row(D_MODEL),
        out_shape=jax.ShapeDtypeStruct((m, D_MODEL), F32),
        scratch_shapes=[pltpu.VMEM((tm, D_FF), BF16)],
        compiler_params=_params(1),
        name="ffn_block",
    )(x2, x2, x2, p2, wg, wu, cw, cb, wd, gain, bias, wpg, wpp)


def _compact_rope_tables(positions):
    inv = lambda rot: 1.0 / (ROPE_THETA ** (jnp.arange(0, rot, 2, dtype=F32) / rot))
    used = A_ROT // 2 + B_ROPE // 2
    inv_freq = jnp.concatenate([inv(A_ROT), inv(B_ROPE), jnp.zeros((LANES - used,), F32)])
    ang = positions.reshape(-1, 1).astype(F32) * inv_freq[None, :]
    return jnp.cos(ang), jnp.sin(ang)


def _ab_weights(w_in, w_q_up, w_kv_up):
    o3 = 3 * A_WIDTH
    o5 = o3 + B_Q_LORA + B_KV_LORA
    k_pe = jnp.pad(w_in[:, o5:], ((0, 0), (B_NOPE, LANES - B_NOPE - B_ROPE)))
    wall = jnp.concatenate([w_in[:, :o5], k_pe], axis=1).astype(BF16)
    per_head = B_NOPE + B_ROPE
    wq = jnp.pad(w_q_up.reshape(B_Q_LORA, B_HEADS, per_head),
                 ((0, 0), (0, 0), (0, LANES - per_head))).reshape(B_Q_LORA, B_HEADS * LANES)
    kv = w_kv_up.reshape(B_KV_LORA, B_HEADS, B_NOPE + B_V)
    wk = jnp.pad(kv[:, :, :B_NOPE], ((0, 0), (0, 0), (0, LANES - B_NOPE)))
    wkv = jnp.concatenate([wk.reshape(B_KV_LORA, B_HEADS * LANES),
                           kv[:, :, B_NOPE:].reshape(B_KV_LORA, B_HEADS * B_V)], axis=1)
    return wall, wq.astype(BF16), wkv.astype(BF16)


def kernel(x, p, positions, ab_w_in, ab_q_norm, ab_w_q_up, ab_kv_norm, ab_w_kv_up, ab_w_out,
           c_w_qkv, c_lambda, c_subln, c_w_out, ln_mix_g, ln_mix_b, ffn_w_gate, ffn_w_up,
           ffn_conv_w, ffn_conv_b, ffn_w_down, ln_ffn_g, ln_ffn_b, ple_w_gate, ple_w_proj):
    batch, seq, d = x.shape
    m = batch * seq
    cos_c, sin_c = _compact_rope_tables(positions)
    x2 = x.reshape(m, d)
    row_vec = lambda v: v.reshape(1, -1)
    for i in range(DEPTH):
        j = i // 2
        if i % 2 == 0:
            wall, wq, wkv = _ab_weights(ab_w_in[j], ab_w_q_up[j], ab_w_kv_up[j])
            qa, ka, va, qb, kb, vb = _ab_in_proj(
                x2, wall, row_vec(ab_q_norm[j]), wq, row_vec(ab_kv_norm[j]), wkv, cos_c, sin_c)
            out_a = _dilated_attention(qa, ka, va, batch, seq)
            out_b = _mla_attention(qb, kb, vb, batch, seq)
            w_out = ab_w_out[j].astype(BF16)
            acts, weights = [out_a, out_b], [w_out[:A_WIDTH], w_out[A_WIDTH:]]
        else:
            lambda_init = 0.8 - 0.6 * math.exp(-0.3 * i)
            q, k, v = _c_in_proj(x2, c_w_qkv[j].astype(BF16), cos_c, sin_c)
            subln = row_vec(c_subln[j])
            acts = [_diff_attention(c_lambda[j], subln, q, k, v, batch, seq, lambda_init)]
            weights = [c_w_out[j].astype(BF16)]
        x2 = _out_proj_ln(x2, acts, weights, row_vec(ln_mix_g[i]), row_vec(ln_mix_b[i]))
        x2 = _ffn_block(x2, p[i].reshape(m, PLE_DIM), ffn_w_gate[i].astype(BF16),
                        ffn_w_up[i].astype(BF16), ffn_conv_w[i], row_vec(ffn_conv_b[i]),
                        ffn_w_down[i].astype(BF16), row_vec(ln_ffn_g[i]), row_vec(ln_ffn_b[i]),
                        ple_w_gate[i].astype(BF16), ple_w_proj[i].astype(BF16), seq)
    return x2.reshape(batch, seq, d)
```

```python
import functools
import math

import numpy as np
import jax
import jax.numpy as jnp
from jax import lax
from jax.experimental import pallas as pl
from jax.experimental.pallas import tpu as pltpu

F32 = jnp.float32
BF16 = jnp.bfloat16

D_MODEL = 1024
DEPTH = 2
PLE_DIM = 256
ROPE_THETA = 500000.0
A_HEAD_DIM = 64
A_HEADS = 8
A_ROT = 16
A_PATTERNS = ((128, 1), (512, 4), (2048, 16))
A_SIDE = 64
B_HEADS = 8
B_Q_LORA = 384
B_KV_LORA = 256
B_NOPE = 64
B_ROPE = 32
B_V = 64
C_HEAD_DIM = 64
C_HEADS = 8
C_ROT = 16
D_FF = 2816
LN_EPS = 1e-5
RMS_EPS = 1e-6
NEG_INF = -1e30
ALPHA = (2 * DEPTH) ** 0.25
A_WIDTH = A_HEADS * A_HEAD_DIM
LOG2E = 1.4426950408889634

LANES = 128
TABLE_GROUPS = 4
TABLE_GROUP_LANES = LANES // TABLE_GROUPS
VMEM_LIMIT = 60 * 1024 * 1024

PROJ_ROWS = 512
FFN_ROWS = 1024
FFN_CHUNK = 256
HALO = 16
ATT_Q_ROWS = 256
ATT_UNROLL = 8
DIL_Q_ROWS = 128


def _params(n_axes):
    return pltpu.CompilerParams(
        dimension_semantics=("arbitrary",) * n_axes, vmem_limit_bytes=VMEM_LIMIT)


def _dot(a, b):
    return jnp.dot(a, b, preferred_element_type=F32)


def _dot_nt(a, b):
    return lax.dot_general(a, b, (((1,), (1,)), ((), ())), preferred_element_type=F32)


def _rope_group(x, cos_t, sin_t, half, take_upper):
    upper = pltpu.roll(x, LANES - half, axis=1)
    lower = pltpu.roll(x, half, axis=1)
    return x * cos_t + jnp.where(take_upper, upper, lower) * sin_t


def _rope_wide(x, cos_t, sin_t, half, take_upper):
    groups = [
        _rope_group(x[:, g * LANES:(g + 1) * LANES], cos_t, sin_t, half, take_upper)
        for g in range(x.shape[1] // LANES)
    ]
    return jnp.concatenate(groups, axis=1)


def _rope_lane_tables(cos_c, sin_c, rot, rot_start, period, base):
    rel = _lane_iota() % period - rot_start
    rotary = (rel >= 0) & (rel < rot)
    first = rel < rot // 2
    freq = jnp.where(first, rel, rel - rot // 2)
    idx = jnp.broadcast_to(jnp.where(rotary, base + freq, 0), cos_c.shape)
    cos_t = jnp.where(rotary, jnp.take_along_axis(cos_c, idx, axis=1), 1.0)
    sin_g = jnp.take_along_axis(sin_c, idx, axis=1)
    sin_t = jnp.where(rotary, jnp.where(first, -sin_g, sin_g), 0.0)
    return cos_t, sin_t


def _rms_norm(h, gain, eps):
    ms = jnp.mean(h * h, axis=-1, keepdims=True)
    return h * lax.rsqrt(ms + eps) * gain


def _layer_norm(z, gain, bias):
    mu = jnp.mean(z, axis=-1, keepdims=True)
    zc = z - mu
    var = jnp.mean(zc * zc, axis=-1, keepdims=True)
    return zc * lax.rsqrt(var + LN_EPS) * gain + bias


def _lane_iota():
    return lax.broadcasted_iota(jnp.int32, (1, LANES), 1)


def _softmax_pv(s, v_aug, c):
    m = jnp.max(s, axis=-1, keepdims=True)
    p = jnp.exp2((s - m) * c).astype(BF16)
    return _dot(p, v_aug), m


def _table_base(tiles_per_group):
    return lax.div(pl.program_id(0), tiles_per_group) * TABLE_GROUP_LANES


def _ab_in_kernel(x_ref, wall_ref, qn_ref, wq_ref, kvn_ref, wkv_ref, cos_ref, sin_ref,
                  qa_ref, ka_ref, va_ref, qb_ref, kb_ref, vb_ref, *, tiles_per_group):
    lane = _lane_iota()
    upper_a = (lane % A_HEAD_DIM) < (A_ROT // 2)
    upper_b = (lane >= B_NOPE) & (lane < B_NOPE + B_ROPE // 2)
    xb = x_ref[...].astype(BF16)
    h = _dot(xb, wall_ref[...])
    cos_c, sin_c = cos_ref[...], sin_ref[...]
    base = _table_base(tiles_per_group)
    ca, sa = _rope_lane_tables(cos_c, sin_c, A_ROT, 0, A_HEAD_DIM, base)
    cb, sb = _rope_lane_tables(cos_c, sin_c, B_ROPE, B_NOPE, LANES, base + A_ROT // 2)
    o1, o2, o3 = A_WIDTH, 2 * A_WIDTH, 3 * A_WIDTH
    o4 = o3 + B_Q_LORA
    o5 = o4 + B_KV_LORA
    qa_ref[...] = _rope_wide(h[:, :o1], ca, sa, A_ROT // 2, upper_a)
    ka_ref[...] = _rope_wide(h[:, o1:o2], ca, sa, A_ROT // 2, upper_a)
    va_ref[...] = h[:, o2:o3]
    cq = _rms_norm(h[:, o3:o4], qn_ref[...], RMS_EPS).astype(BF16)
    qb = _dot(cq, wq_ref[...])
    qb_ref[...] = _rope_wide(qb, cb, sb, B_ROPE // 2, upper_b).astype(BF16)
    ckv = _rms_norm(h[:, o4:o5], kvn_ref[...], RMS_EPS).astype(BF16)
    kv = _dot(ckv, wkv_ref[...])
    kpe = _rope_group(h[:, o5:o5 + LANES], cb, sb, B_ROPE // 2, upper_b)
    kslots = B_HEADS * LANES
    kb_ref[...] = (kv[:, :kslots] + jnp.concatenate([kpe] * B_HEADS, axis=1)).astype(BF16)
    vb_ref[...] = kv[:, kslots:].astype(BF16)


def _table_spec(m, tm):
    group_rows = m // TABLE_GROUPS
    assert group_rows % tm == 0
    tiles_per_group = group_rows // tm
    return tiles_per_group, pl.BlockSpec((tm, LANES), lambda i: (i % tiles_per_group, 0))


def _ab_in_proj(x2, wall, qn, wq, kvn, wkv, cos_c, sin_c):
    m = x2.shape[0]
    tm = PROJ_ROWS
    row = lambda w: pl.BlockSpec((tm, w), lambda i: (i, 0))
    full = lambda a: pl.BlockSpec(a.shape, lambda i: (0, 0))
    out_shape = (
        jax.ShapeDtypeStruct((m, A_WIDTH), F32),
        jax.ShapeDtypeStruct((m, A_WIDTH), F32),
        jax.ShapeDtypeStruct((m, A_WIDTH), F32),
        jax.ShapeDtypeStruct((m, B_HEADS * LANES), BF16),
        jax.ShapeDtypeStruct((m, B_HEADS * LANES), BF16),
        jax.ShapeDtypeStruct((m, B_HEADS * B_V), BF16),
    )
    tiles_per_group, table_spec = _table_spec(m, tm)
    return pl.pallas_call(
        functools.partial(_ab_in_kernel, tiles_per_group=tiles_per_group),
        grid=(m // tm,),
        in_specs=[row(D_MODEL), full(wall), full(qn), full(wq), full(kvn), full(wkv),
                  table_spec, table_spec],
        out_specs=(row(A_WIDTH), row(A_WIDTH), row(A_WIDTH), row(B_HEADS * LANES),
                   row(B_HEADS * LANES), row(B_HEADS * B_V)),
        out_shape=out_shape,
        compiler_params=_params(1),
        name="ab_in_proj",
    )(x2, wall, qn, wq, kvn, wkv, cos_c, sin_c)


def _band_bias(tq, tk, offset):
    delta = np.arange(tk)[None, :] - np.arange(tq)[:, None] + offset
    bias = np.where(np.abs(delta) <= A_SIDE, 0.0, NEG_INF).astype(np.float32)
    return np.concatenate([bias, bias], axis=0)


def _window_start(q0, length, tk):
    return min(max(q0 - A_SIDE, 0), length - tk)


def _dilated_kernel(q_ref, k_ref, v_ref, bias_w_ref, bias_s_ref, o_ref, *stat_refs):
    seq = q_ref.shape[0]
    tq = DIL_Q_ROWS
    c = (A_HEAD_DIM ** -0.5) * LOG2E
    head0 = _lane_iota() < A_HEAD_DIM

    for pat, (window, dil) in enumerate(A_PATTERNS):
        assert window // (2 * dil) == A_SIDE
        num_ref, m_ref, l_ref = stat_refs[3 * pat:3 * pat + 3]
        length = seq // dil
        tk = min(tq + 2 * A_SIDE, length)
        ones = jnp.ones((tk, LANES), BF16)
        for res in range(dil):
            for q0 in range(0, length, tq):
                ws = _window_start(q0, length, tk)
                if dil == 1:
                    qrows, krows = pl.ds(q0, tq), pl.ds(ws, tk)
                else:
                    qrows = pl.ds(res + q0 * dil, tq, stride=dil)
                    krows = pl.ds(res + ws * dil, tk, stride=dil)
                if tk == length:
                    bias = bias_s_ref[...]
                else:
                    bias = bias_w_ref[(0, A_SIDE, 2 * A_SIDE).index(q0 - ws)]
                q = q_ref[qrows, :]
                ql = jnp.concatenate(
                    [jnp.where(head0, q, 0.0), jnp.where(head0, 0.0, q)], axis=0).astype(BF16)
                k = k_ref[krows, :].astype(BF16)
                v_aug = jnp.concatenate([v_ref[krows, :].astype(BF16), ones], axis=1)
                r, m = _softmax_pv(_dot_nt(ql, k) + bias, v_aug, c)
                num_ref[qrows, :] = jnp.where(head0, r[:tq, :LANES], r[tq:, :LANES])
                l_ref[qrows, :] = jnp.where(head0, r[:tq, LANES:], r[tq:, LANES:])
                m_ref[qrows, :] = jnp.where(head0, m[:tq], m[tq:])

    rows = 2 * tq
    n_pat = len(A_PATTERNS)
    for r0 in range(0, seq, rows):
        sl = pl.ds(r0, rows)
        ms = [stat_refs[3 * pat + 1][sl, :] for pat in range(n_pat)]
        m_all = functools.reduce(jnp.maximum, ms)
        scale = [jnp.exp2((mp - m_all) * c) for mp in ms]
        num = sum(stat_refs[3 * pat][sl, :] * scale[pat] for pat in range(n_pat))
        den = sum(stat_refs[3 * pat + 2][sl, :] * scale[pat] for pat in range(n_pat))
        o_ref[sl, :] = (num / den).astype(o_ref.dtype)


def _dilated_attention(qa, ka, va, batch, seq):
    tq = DIL_Q_ROWS
    q3, k3, v3 = (t.reshape(batch, seq, A_WIDTH) for t in (qa, ka, va))
    tkw = tq + 2 * A_SIDE
    bias_w = jnp.asarray(np.stack([_band_bias(tq, tkw, -off) for off in (0, A_SIDE, 2 * A_SIDE)]))
    bias_s = jnp.asarray(_band_bias(tq, tq, 0))
    spec = pl.BlockSpec((None, seq, LANES), lambda b, j: (b, 0, j))
    full = lambda a: pl.BlockSpec(a.shape, lambda b, j: (0,) * a.ndim)
    out = pl.pallas_call(
        _dilated_kernel,
        grid=(batch, A_WIDTH // LANES),
        in_specs=[spec, spec, spec, full(bias_w), full(bias_s)],
        out_specs=spec,
        out_shape=jax.ShapeDtypeStruct((batch, seq, A_WIDTH), BF16),
        scratch_shapes=[pltpu.VMEM((seq, LANES), F32)] * (3 * len(A_PATTERNS)),
        compiler_params=_params(2),
        name="dilated_attention",
    )(q3, k3, v3, bias_w, bias_s)
    return out.reshape(batch * seq, A_WIDTH)


def _mla_attn_kernel(q_ref, k_ref, v_ref, o_ref, vaug_ref):
    seq = q_ref.shape[0]
    tq = ATT_Q_ROWS
    c = ((B_NOPE + B_ROPE) ** -0.5) * LOG2E
    head0 = _lane_iota() < B_V
    vaug_ref[:, :LANES] = v_ref[...]
    vaug_ref[:, LANES:] = jnp.ones((seq, LANES), BF16)

    def tile(i, carry):
        rows = pl.ds(pl.multiple_of(i * tq, tq), tq)
        outs = []
        for h in range(2):
            q = q_ref[rows, h * LANES:(h + 1) * LANES]
            k = k_ref[:, h * LANES:(h + 1) * LANES]
            r, _ = _softmax_pv(_dot_nt(q, k), vaug_ref[...], c)
            outs.append(r[:, :LANES] / r[:, LANES:])
        o_ref[rows, :] = jnp.where(head0, outs[0], outs[1]).astype(o_ref.dtype)
        return carry

    lax.fori_loop(0, seq // tq, tile, 0, unroll=ATT_UNROLL)


def _mla_attention(qb, kb, vb, batch, seq):
    q3 = qb.reshape(batch, seq, B_HEADS * LANES)
    k3 = kb.reshape(batch, seq, B_HEADS * LANES)
    v3 = vb.reshape(batch, seq, B_HEADS * B_V)
    qk_spec = pl.BlockSpec((None, seq, 2 * LANES), lambda b, j: (b, 0, j))
    v_spec = pl.BlockSpec((None, seq, LANES), lambda b, j: (b, 0, j))
    out = pl.pallas_call(
        _mla_attn_kernel,
        grid=(batch, B_HEADS // 2),
        in_specs=[qk_spec, qk_spec, v_spec],
        out_specs=v_spec,
        out_shape=jax.ShapeDtypeStruct((batch, seq, B_HEADS * B_V), BF16),
        scratch_shapes=[pltpu.VMEM((seq, 2 * LANES), BF16)],
        compiler_params=_params(2),
        name="mla_attention",
    )(q3, k3, v3)
    return out.reshape(batch * seq, B_HEADS * B_V)


def _diff_attn_kernel(lam_ref, subln_ref, q_ref, k_ref, v_ref, o_ref, vaug_ref, *, lambda_init):
    seq = q_ref.shape[0]
    tq = ATT_Q_ROWS
    c = (C_HEAD_DIM ** -0.5) * LOG2E
    lp = lam_ref[...]
    t1 = jnp.sum(lp[0:1, :] * lp[1:2, :], axis=-1, keepdims=True)
    t2 = jnp.sum(lp[2:3, :] * lp[3:4, :], axis=-1, keepdims=True)
    lam = jnp.exp(t1) - jnp.exp(t2) + lambda_init
    comp0 = _lane_iota() < C_HEAD_DIM
    gain = subln_ref[...] * (1.0 - lambda_init)
    vaug_ref[:, :LANES] = v_ref[...]
    vaug_ref[:, LANES:] = jnp.ones((seq, LANES), BF16)

    def tile(i, carry):
        rows = pl.ds(pl.multiple_of(i * tq, tq), tq)
        q = q_ref[rows, :]
        zero = jnp.zeros_like(q)
        ql = jnp.concatenate([jnp.where(comp0, q, zero), jnp.where(comp0, zero, q)], axis=0)
        r, _ = _softmax_pv(_dot_nt(ql, k_ref[...]), vaug_ref[...], c)
        o = r[:tq, :LANES] / r[:tq, LANES:] - lam * (r[tq:, :LANES] / r[tq:, LANES:])
        o_ref[rows, :] = _rms_norm(o, gain, LN_EPS).astype(o_ref.dtype)
        return carry

    lax.fori_loop(0, seq // tq, tile, 0, unroll=ATT_UNROLL)


def _diff_attention(lam_params, subln, q, k, v, batch, seq, lambda_init):
    width = C_HEADS * 2 * C_HEAD_DIM
    q3, k3, v3 = (t.reshape(batch, seq, width) for t in (q, k, v))
    full = lambda a: pl.BlockSpec(a.shape, lambda b, h: (0, 0))
    spec = pl.BlockSpec((None, seq, LANES), lambda b, h: (b, 0, h))
    out = pl.pallas_call(
        functools.partial(_diff_attn_kernel, lambda_init=lambda_init),
        grid=(batch, C_HEADS),
        in_specs=[full(lam_params), full(subln), spec, spec, spec],
        out_specs=spec,
        out_shape=jax.ShapeDtypeStruct((batch, seq, width), BF16),
        scratch_shapes=[pltpu.VMEM((seq, 2 * LANES), BF16)],
        compiler_params=_params(2),
        name="diff_attention",
    )(lam_params, subln, q3, k3, v3)
    return out.reshape(batch * seq, width)


def _c_in_kernel(x_ref, w_ref, cos_ref, sin_ref, q_ref, k_ref, v_ref, *, tiles_per_group):
    upper = (_lane_iota() % C_HEAD_DIM) < (C_ROT // 2)
    width = q_ref.shape[1]
    h = _dot(x_ref[...].astype(BF16), w_ref[...])
    ca, sa = _rope_lane_tables(cos_ref[...], sin_ref[...], C_ROT, 0, C_HEAD_DIM,
                               _table_base(tiles_per_group))
    q_ref[...] = _rope_wide(h[:, :width], ca, sa, C_ROT // 2, upper).astype(BF16)
    k_ref[...] = _rope_wide(h[:, width:2 * width], ca, sa, C_ROT // 2, upper).astype(BF16)
    v_ref[...] = h[:, 2 * width:].astype(BF16)


def _c_in_proj(x2, w, cos_c, sin_c):
    m = x2.shape[0]
    tm = PROJ_ROWS
    width = w.shape[1] // 3
    row = lambda n: pl.BlockSpec((tm, n), lambda i: (i, 0))
    out = jax.ShapeDtypeStruct((m, width), BF16)
    tiles_per_group, table_spec = _table_spec(m, tm)
    return pl.pallas_call(
        functools.partial(_c_in_kernel, tiles_per_group=tiles_per_group),
        grid=(m // tm,),
        in_specs=[row(D_MODEL), pl.BlockSpec(w.shape, lambda i: (0, 0)), table_spec, table_spec],
        out_specs=(row(width), row(width), row(width)),
        out_shape=(out, out, out),
        compiler_params=_params(1),
        name="c_in_proj",
    )(x2, w, cos_c, sin_c)


def _gelu_tanh(c):
    return 0.5 * c * (1.0 + jnp.tanh(math.sqrt(2.0 / math.pi) * (c + 0.044715 * (c * c * c))))


def _layer_tail_kernel(*refs, n_in, tiles_per_seq):
    x_refs = refs[0:3]
    a_refs = refs[3:3 + 3 * n_in]
    w_refs = refs[3 + 3 * n_in:3 + 4 * n_in]
    (gm_ref, bm_ref, p_ref, wg_ref, wu_ref, cw_ref, cb_ref, wd_ref, g_ref, b_ref, wpg_ref,
     wpp_ref, o_ref, g_scr) = refs[3 + 4 * n_in:]
    tm = x_refs[0].shape[0]
    tn = FFN_CHUNK
    n_chunks = wg_ref.shape[1] // tn
    rows = tm + 2 * HALO
    pos = lax.rem(pl.program_id(0), tiles_per_seq)
    stack = lambda main, prev, nxt: jnp.concatenate([prev[...], main[...], nxt[...]], axis=0)

    y = None
    for k in range(n_in):
        d = _dot(stack(*a_refs[3 * k:3 * k + 3]), w_refs[k][...])
        y = d if y is None else y + d
    x1 = _layer_norm(ALPHA * stack(*x_refs) + y, gm_ref[...], bm_ref[...])
    r = lax.broadcasted_iota(jnp.int32, (rows, 1), 0)
    inside = ((r >= HALO) | (pos != 0)) & ((r < HALO + tm) | (pos != tiles_per_seq - 1))
    xh = jnp.where(inside, x1, 0.0).astype(BF16)
    xb = xh[HALO:HALO + tm]
    x = x1[HALO:HALO + tm]
    for n in range(n_chunks):
        cols = slice(n * tn, (n + 1) * tn)
        a = _dot(xh, wg_ref[:, cols])
        u = _dot(xb, wu_ref[:, cols])
        cw = cw_ref[:, cols]
        a_prev = pltpu.roll(a, 1, axis=0)[HALO:HALO + tm]
        a_next = pltpu.roll(a, rows - 1, axis=0)[HALO:HALO + tm]
        conv = (cb_ref[:, cols] + cw[0:1, :] * a_prev + cw[1:2, :] * a[HALO:HALO + tm]
                + cw[2:3, :] * a_next)
        g_scr[:, cols] = (_gelu_tanh(conv) * u).astype(BF16)
    f = _dot(g_scr[...], wd_ref[...])
    x2 = _layer_norm(ALPHA * x + f, g_ref[...], b_ref[...])
    gate = jax.nn.sigmoid(_dot(x2.astype(BF16), wpg_ref[...]))
    proj = _dot(p_ref[...].astype(BF16), wpp_ref[...])
    o_ref[...] = x2 + gate * proj


def _layer_tail(x2, acts, w_outs, gain_mix, bias_mix, p2, wg, wu, cw, cb, wd, gain, bias, wpg,
                wpp, seq):
    m = x2.shape[0]
    tm = FFN_ROWS
    halo_blocks = tm // HALO
    last_halo = m // HALO - 1

    def tiled(width):
        return [pl.BlockSpec((tm, width), lambda i: (i, 0)),
                pl.BlockSpec((HALO, width), lambda i: (jnp.maximum(i * halo_blocks - 1, 0), 0)),
                pl.BlockSpec((HALO, width),
                             lambda i: (jnp.minimum((i + 1) * halo_blocks, last_halo), 0))]

    def resident(a):
        nd = a.ndim
        return pl.BlockSpec(a.shape, lambda i: (0,) * nd, pipeline_mode=pl.Buffered(1))

    residents = [*w_outs, gain_mix, bias_mix]
    ffn_residents = [wg, wu, cw, cb, wd, gain, bias, wpg, wpp]
    in_specs = (tiled(D_MODEL) + [spec for a in acts for spec in tiled(a.shape[1])]
                + [resident(a) for a in residents]
                + [pl.BlockSpec((tm, PLE_DIM), lambda i: (i, 0))]
                + [resident(a) for a in ffn_residents])
    operands = ([x2] * 3 + [a for a in acts for _ in range(3)] + residents + [p2]
                + ffn_residents)
    return pl.pallas_call(
        functools.partial(_layer_tail_kernel, n_in=len(acts), tiles_per_seq=seq // tm),
        grid=(m // tm,),
        in_specs=in_specs,
        out_specs=pl.BlockSpec((tm, D_MODEL), lambda i: (i, 0)),
        out_shape=jax.ShapeDtypeStruct((m, D_MODEL), F32),
        scratch_shapes=[pltpu.VMEM((tm, D_FF), BF16)],
        compiler_params=_params(1),
        name="layer_tail",
    )(*operands)


def _compact_rope_tables(positions):
    inv = lambda rot: 1.0 / (ROPE_THETA ** (jnp.arange(0, rot, 2, dtype=F32) / rot))
    used = A_ROT // 2 + B_ROPE // 2
    inv_freq = jnp.concatenate(
        [inv(A_ROT), inv(B_ROPE), jnp.zeros((TABLE_GROUP_LANES - used,), F32)])
    pos = positions.reshape(TABLE_GROUPS, -1).astype(F32).T
    ang = (pos[:, :, None] * inv_freq[None, None, :]).reshape(pos.shape[0], LANES)
    return jnp.cos(ang), jnp.sin(ang)


def _ab_weights(w_in, w_q_up, w_kv_up):
    o3 = 3 * A_WIDTH
    o5 = o3 + B_Q_LORA + B_KV_LORA
    k_pe = jnp.pad(w_in[:, o5:], ((0, 0), (B_NOPE, LANES - B_NOPE - B_ROPE)))
    wall = jnp.concatenate([w_in[:, :o5], k_pe], axis=1).astype(BF16)
    per_head = B_NOPE + B_ROPE
    wq = jnp.pad(w_q_up.reshape(B_Q_LORA, B_HEADS, per_head),
                 ((0, 0), (0, 0), (0, LANES - per_head))).reshape(B_Q_LORA, B_HEADS * LANES)
    kv = w_kv_up.reshape(B_KV_LORA, B_HEADS, B_NOPE + B_V)
    wk = jnp.pad(kv[:, :, :B_NOPE], ((0, 0), (0, 0), (0, LANES - B_NOPE)))
    wkv = jnp.concatenate([wk.reshape(B_KV_LORA, B_HEADS * LANES),
                           kv[:, :, B_NOPE:].reshape(B_KV_LORA, B_HEADS * B_V)], axis=1)
    return wall, wq.astype(BF16), wkv.astype(BF16)


def kernel(x, p, positions, ab_w_in, ab_q_norm, ab_w_q_up, ab_kv_norm, ab_w_kv_up, ab_w_out,
           c_w_qkv, c_lambda, c_subln, c_w_out, ln_mix_g, ln_mix_b, ffn_w_gate, ffn_w_up,
           ffn_conv_w, ffn_conv_b, ffn_w_down, ln_ffn_g, ln_ffn_b, ple_w_gate, ple_w_proj):
    batch, seq, d = x.shape
    m = batch * seq
    cos_c, sin_c = _compact_rope_tables(positions)
    x2 = x.reshape(m, d)
    row_vec = lambda v: v.reshape(1, -1)
    for i in range(DEPTH):
        j = i // 2
        if i % 2 == 0:
            wall, wq, wkv = _ab_weights(ab_w_in[j], ab_w_q_up[j], ab_w_kv_up[j])
            qa, ka, va, qb, kb, vb = _ab_in_proj(
                x2, wall, row_vec(ab_q_norm[j]), wq, row_vec(ab_kv_norm[j]), wkv, cos_c, sin_c)
            out_a = _dilated_attention(qa, ka, va, batch, seq)
            out_b = _mla_attention(qb, kb, vb, batch, seq)
            w_out = ab_w_out[j].astype(BF16)
            acts, weights = [out_a, out_b], [w_out[:A_WIDTH], w_out[A_WIDTH:]]
        else:
            lambda_init = 0.8 - 0.6 * math.exp(-0.3 * i)
            q, k, v = _c_in_proj(x2, c_w_qkv[j].astype(BF16), cos_c, sin_c)
            subln = row_vec(c_subln[j])
            acts = [_diff_attention(c_lambda[j], subln, q, k, v, batch, seq, lambda_init)]
            weights = [c_w_out[j].astype(BF16)]
        x2 = _layer_tail(x2, acts, weights, row_vec(ln_mix_g[i]), row_vec(ln_mix_b[i]),
                         p[i].reshape(m, PLE_DIM), ffn_w_gate[i].astype(BF16),
                         ffn_w_up[i].astype(BF16), ffn_conv_w[i], row_vec(ffn_conv_b[i]),
                         ffn_w_down[i].astype(BF16), row_vec(ln_ffn_g[i]), row_vec(ln_ffn_b[i]),
                         ple_w_gate[i].astype(BF16), ple_w_proj[i].astype(BF16), seq)
    return x2.reshape(batch, seq, d)
```

```python
import functools
import math

import numpy as np
import jax
import jax.numpy as jnp
from jax import lax
from jax.experimental import pallas as pl
from jax.experimental.pallas import tpu as pltpu

F32 = jnp.float32
BF16 = jnp.bfloat16

D_MODEL = 1024
DEPTH = 2
PLE_DIM = 256
ROPE_THETA = 500000.0
A_HEAD_DIM = 64
A_HEADS = 8
A_ROT = 16
A_PATTERNS = ((128, 1), (512, 4), (2048, 16))
A_SIDE = 64
B_HEADS = 8
B_Q_LORA = 384
B_KV_LORA = 256
B_NOPE = 64
B_ROPE = 32
B_V = 64
C_HEAD_DIM = 64
C_HEADS = 8
C_ROT = 16
D_FF = 2816
LN_EPS = 1e-5
RMS_EPS = 1e-6
NEG_INF = -1e30
ALPHA = (2 * DEPTH) ** 0.25
A_WIDTH = A_HEADS * A_HEAD_DIM
LOG2E = 1.4426950408889634

LANES = 128
TABLE_GROUPS = 4
TABLE_GROUP_LANES = LANES // TABLE_GROUPS
VMEM_LIMIT = 60 * 1024 * 1024

PROJ_ROWS = 512
FFN_ROWS = 1024
FFN_CHUNK = 256
HALO = 16
ATT_Q_ROWS = 256
ATT_UNROLL = 8
DIL_Q_ROWS = 128


def _params(n_axes):
    return pltpu.CompilerParams(
        dimension_semantics=("arbitrary",) * n_axes, vmem_limit_bytes=VMEM_LIMIT)


def _dot(a, b):
    return jnp.dot(a, b, preferred_element_type=F32)


def _dot_nt(a, b):
    return lax.dot_general(a, b, (((1,), (1,)), ((), ())), preferred_element_type=F32)


def _rope_group(x, cos_t, sin_t, half, take_upper):
    upper = pltpu.roll(x, LANES - half, axis=1)
    lower = pltpu.roll(x, half, axis=1)
    return x * cos_t + jnp.where(take_upper, upper, lower) * sin_t


def _rope_wide(x, cos_t, sin_t, half, take_upper):
    groups = [
        _rope_group(x[:, g * LANES:(g + 1) * LANES], cos_t, sin_t, half, take_upper)
        for g in range(x.shape[1] // LANES)
    ]
    return jnp.concatenate(groups, axis=1)


def _rope_lane_tables(cos_c, sin_c, rot, rot_start, period, base):
    rel = _lane_iota() % period - rot_start
    rotary = (rel >= 0) & (rel < rot)
    first = rel < rot // 2
    freq = jnp.where(first, rel, rel - rot // 2)
    idx = jnp.broadcast_to(jnp.where(rotary, base + freq, 0), cos_c.shape)
    cos_t = jnp.where(rotary, jnp.take_along_axis(cos_c, idx, axis=1), 1.0)
    sin_g = jnp.take_along_axis(sin_c, idx, axis=1)
    sin_t = jnp.where(rotary, jnp.where(first, -sin_g, sin_g), 0.0)
    return cos_t, sin_t


def _rms_norm(h, gain, eps):
    ms = jnp.mean(h * h, axis=-1, keepdims=True)
    return h * lax.rsqrt(ms + eps) * gain


def _layer_norm(z, gain, bias):
    mu = jnp.mean(z, axis=-1, keepdims=True)
    zc = z - mu
    var = jnp.mean(zc * zc, axis=-1, keepdims=True)
    return zc * lax.rsqrt(var + LN_EPS) * gain + bias


def _lane_iota():
    return lax.broadcasted_iota(jnp.int32, (1, LANES), 1)


def _softmax_pv(s, v_aug, c):
    m = jnp.max(s, axis=-1, keepdims=True)
    p = jnp.exp2((s - m) * c).astype(BF16)
    return _dot(p, v_aug), m


def _table_base(tiles_per_group):
    return lax.div(pl.program_id(0), tiles_per_group) * TABLE_GROUP_LANES


def _ab_in_kernel(x_ref, wmain_ref, wkpe_ref, qn_ref, wq_ref, kvn_ref, wkv_ref, cos_ref, sin_ref,
                  qa_ref, ka_ref, va_ref, qb_ref, kb_ref, vb_ref, *, tiles_per_group):
    lane = _lane_iota()
    upper_a = (lane % A_HEAD_DIM) < (A_ROT // 2)
    upper_b = (lane >= B_NOPE) & (lane < B_NOPE + B_ROPE // 2)
    xb = x_ref[...].astype(BF16)
    h = _dot(xb, wmain_ref[...])
    cos_c, sin_c = cos_ref[...], sin_ref[...]
    base = _table_base(tiles_per_group)
    ca, sa = _rope_lane_tables(cos_c, sin_c, A_ROT, 0, A_HEAD_DIM, base)
    cb, sb = _rope_lane_tables(cos_c, sin_c, B_ROPE, B_NOPE, LANES, base + A_ROT // 2)
    o1, o2, o3 = A_WIDTH, 2 * A_WIDTH, 3 * A_WIDTH
    o4 = o3 + B_Q_LORA
    o5 = o4 + B_KV_LORA
    qa_ref[...] = _rope_wide(h[:, :o1], ca, sa, A_ROT // 2, upper_a)
    ka_ref[...] = _rope_wide(h[:, o1:o2], ca, sa, A_ROT // 2, upper_a)
    va_ref[...] = h[:, o2:o3]
    cq = _rms_norm(h[:, o3:o4], qn_ref[...], RMS_EPS).astype(BF16)
    qb = _dot(cq, wq_ref[...])
    qb_ref[...] = _rope_wide(qb, cb, sb, B_ROPE // 2, upper_b).astype(BF16)
    ckv = _rms_norm(h[:, o4:o5], kvn_ref[...], RMS_EPS).astype(BF16)
    kv = _dot(ckv, wkv_ref[...])
    kpe = _rope_group(_dot(xb, wkpe_ref[...]), cb, sb, B_ROPE // 2, upper_b)
    kslots = B_HEADS * LANES
    kb_ref[...] = (kv[:, :kslots] + jnp.concatenate([kpe] * B_HEADS, axis=1)).astype(BF16)
    vb_ref[...] = kv[:, kslots:].astype(BF16)


def _table_spec(m, tm):
    group_rows = m // TABLE_GROUPS
    assert group_rows % tm == 0
    tiles_per_group = group_rows // tm
    return tiles_per_group, pl.BlockSpec((tm, LANES), lambda i: (i % tiles_per_group, 0))


def _ab_in_proj(x2, wmain, wkpe, qn, wq, kvn, wkv, cos_c, sin_c):
    m = x2.shape[0]
    tm = PROJ_ROWS
    row = lambda w: pl.BlockSpec((tm, w), lambda i: (i, 0))
    full = lambda a: pl.BlockSpec(a.shape, lambda i: (0, 0))
    out_shape = (
        jax.ShapeDtypeStruct((m, A_WIDTH), F32),
        jax.ShapeDtypeStruct((m, A_WIDTH), F32),
        jax.ShapeDtypeStruct((m, A_WIDTH), F32),
        jax.ShapeDtypeStruct((m, B_HEADS * LANES), BF16),
        jax.ShapeDtypeStruct((m, B_HEADS * LANES), BF16),
        jax.ShapeDtypeStruct((m, B_HEADS * B_V), BF16),
    )
    tiles_per_group, table_spec = _table_spec(m, tm)
    return pl.pallas_call(
        functools.partial(_ab_in_kernel, tiles_per_group=tiles_per_group),
        grid=(m // tm,),
        in_specs=[row(D_MODEL), full(wmain), full(wkpe), full(qn), full(wq), full(kvn), full(wkv),
                  table_spec, table_spec],
        out_specs=(row(A_WIDTH), row(A_WIDTH), row(A_WIDTH), row(B_HEADS * LANES),
                   row(B_HEADS * LANES), row(B_HEADS * B_V)),
        out_shape=out_shape,
        compiler_params=_params(1),
        name="ab_in_proj",
    )(x2, wmain, wkpe, qn, wq, kvn, wkv, cos_c, sin_c)


def _band_bias(tq, tk, offset):
    delta = np.arange(tk)[None, :] - np.arange(tq)[:, None] + offset
    bias = np.where(np.abs(delta) <= A_SIDE, 0.0, NEG_INF).astype(np.float32)
    return np.concatenate([bias, bias], axis=0)


def _window_start(q0, length, tk):
    return min(max(q0 - A_SIDE, 0), length - tk)


def _dilated_kernel(q_ref, k_ref, v_ref, bias_w_ref, bias_s_ref, o_ref, *stat_refs):
    seq = q_ref.shape[0]
    tq = DIL_Q_ROWS
    c = (A_HEAD_DIM ** -0.5) * LOG2E
    head0 = _lane_iota() < A_HEAD_DIM

    for pat, (window, dil) in enumerate(A_PATTERNS):
        assert window // (2 * dil) == A_SIDE
        num_ref, m_ref, l_ref = stat_refs[3 * pat:3 * pat + 3]
        length = seq // dil
        tk = min(tq + 2 * A_SIDE, length)
        ones = jnp.ones((tk, LANES), BF16)
        for res in range(dil):
            for q0 in range(0, length, tq):
                ws = _window_start(q0, length, tk)
                if dil == 1:
                    qrows, krows = pl.ds(q0, tq), pl.ds(ws, tk)
                else:
                    qrows = pl.ds(res + q0 * dil, tq, stride=dil)
                    krows = pl.ds(res + ws * dil, tk, stride=dil)
                if tk == length:
                    bias = bias_s_ref[...]
                else:
                    bias = bias_w_ref[(0, A_SIDE, 2 * A_SIDE).index(q0 - ws)]
                q = q_ref[qrows, :]
                ql = jnp.concatenate(
                    [jnp.where(head0, q, 0.0), jnp.where(head0, 0.0, q)], axis=0).astype(BF16)
                k = k_ref[krows, :].astype(BF16)
                v_aug = jnp.concatenate([v_ref[krows, :].astype(BF16), ones], axis=1)
                r, m = _softmax_pv(_dot_nt(ql, k) + bias, v_aug, c)
                num_ref[qrows, :] = jnp.where(head0, r[:tq, :LANES], r[tq:, :LANES])
                l_ref[qrows, :] = jnp.where(head0, r[:tq, LANES:], r[tq:, LANES:])
                m_ref[qrows, :] = jnp.where(head0, m[:tq], m[tq:])

    rows = 2 * tq
    n_pat = len(A_PATTERNS)
    for r0 in range(0, seq, rows):
        sl = pl.ds(r0, rows)
        ms = [stat_refs[3 * pat + 1][sl, :] for pat in range(n_pat)]
        m_all = functools.reduce(jnp.maximum, ms)
        scale = [jnp.exp2((mp - m_all) * c) for mp in ms]
        num = sum(stat_refs[3 * pat][sl, :] * scale[pat] for pat in range(n_pat))
        den = sum(stat_refs[3 * pat + 2][sl, :] * scale[pat] for pat in range(n_pat))
        o_ref[sl, :] = (num / den).astype(o_ref.dtype)


def _dilated_attention(qa, ka, va, batch, seq):
    tq = DIL_Q_ROWS
    q3, k3, v3 = (t.reshape(batch, seq, A_WIDTH) for t in (qa, ka, va))
    tkw = tq + 2 * A_SIDE
    bias_w = jnp.asarray(np.stack([_band_bias(tq, tkw, -off) for off in (0, A_SIDE, 2 * A_SIDE)]))
    bias_s = jnp.asarray(_band_bias(tq, tq, 0))
    spec = pl.BlockSpec((None, seq, LANES), lambda b, j: (b, 0, j))
    full = lambda a: pl.BlockSpec(a.shape, lambda b, j: (0,) * a.ndim)
    out = pl.pallas_call(
        _dilated_kernel,
        grid=(batch, A_WIDTH // LANES),
        in_specs=[spec, spec, spec, full(bias_w), full(bias_s)],
        out_specs=spec,
        out_shape=jax.ShapeDtypeStruct((batch, seq, A_WIDTH), BF16),
        scratch_shapes=[pltpu.VMEM((seq, LANES), F32)] * (3 * len(A_PATTERNS)),
        compiler_params=_params(2),
        name="dilated_attention",
    )(q3, k3, v3, bias_w, bias_s)
    return out.reshape(batch * seq, A_WIDTH)


def _mla_attn_kernel(q_ref, k_ref, v_ref, o_ref, vaug_ref):
    seq = q_ref.shape[0]
    tq = ATT_Q_ROWS
    c = ((B_NOPE + B_ROPE) ** -0.5) * LOG2E
    head0 = _lane_iota() < B_V
    vaug_ref[:, :LANES] = v_ref[...]
    vaug_ref[:, LANES:] = jnp.ones((seq, LANES), BF16)

    def tile(i, carry):
        rows = pl.ds(pl.multiple_of(i * tq, tq), tq)
        outs = []
        for h in range(2):
            q = q_ref[rows, h * LANES:(h + 1) * LANES]
            k = k_ref[:, h * LANES:(h + 1) * LANES]
            r, _ = _softmax_pv(_dot_nt(q, k), vaug_ref[...], c)
            outs.append(r[:, :LANES] / r[:, LANES:])
        o_ref[rows, :] = jnp.where(head0, outs[0], outs[1]).astype(o_ref.dtype)
        return carry

    lax.fori_loop(0, seq // tq, tile, 0, unroll=ATT_UNROLL)


def _mla_attention(qb, kb, vb, batch, seq):
    q3 = qb.reshape(batch, seq, B_HEADS * LANES)
    k3 = kb.reshape(batch, seq, B_HEADS * LANES)
    v3 = vb.reshape(batch, seq, B_HEADS * B_V)
    qk_spec = pl.BlockSpec((None, seq, 2 * LANES), lambda b, j: (b, 0, j))
    v_spec = pl.BlockSpec((None, seq, LANES), lambda b, j: (b, 0, j))
    out = pl.pallas_call(
        _mla_attn_kernel,
        grid=(batch, B_HEADS // 2),
        in_specs=[qk_spec, qk_spec, v_spec],
        out_specs=v_spec,
        out_shape=jax.ShapeDtypeStruct((batch, seq, B_HEADS * B_V), BF16),
        scratch_shapes=[pltpu.VMEM((seq, 2 * LANES), BF16)],
        compiler_params=_params(2),
        name="mla_attention",
    )(q3, k3, v3)
    return out.reshape(batch * seq, B_HEADS * B_V)


def _diff_attn_kernel(lam_ref, subln_ref, q_ref, k_ref, v_ref, o_ref, vaug_ref, *, lambda_init):
    seq = q_ref.shape[0]
    tq = ATT_Q_ROWS
    c = (C_HEAD_DIM ** -0.5) * LOG2E
    lp = lam_ref[...]
    t1 = jnp.sum(lp[0:1, :] * lp[1:2, :], axis=-1, keepdims=True)
    t2 = jnp.sum(lp[2:3, :] * lp[3:4, :], axis=-1, keepdims=True)
    lam = jnp.exp(t1) - jnp.exp(t2) + lambda_init
    comp0 = _lane_iota() < C_HEAD_DIM
    gain = subln_ref[...] * (1.0 - lambda_init)
    vaug_ref[:, :LANES] = v_ref[...]
    vaug_ref[:, LANES:] = jnp.ones((seq, LANES), BF16)

    def tile(i, carry):
        rows = pl.ds(pl.multiple_of(i * tq, tq), tq)
        q = q_ref[rows, :]
        zero = jnp.zeros_like(q)
        ql = jnp.concatenate([jnp.where(comp0, q, zero), jnp.where(comp0, zero, q)], axis=0)
        r, _ = _softmax_pv(_dot_nt(ql, k_ref[...]), vaug_ref[...], c)
        o = r[:tq, :LANES] / r[:tq, LANES:] - lam * (r[tq:, :LANES] / r[tq:, LANES:])
        o_ref[rows, :] = _rms_norm(o, gain, LN_EPS).astype(o_ref.dtype)
        return carry

    lax.fori_loop(0, seq // tq, tile, 0, unroll=ATT_UNROLL)


def _diff_attention(lam_params, subln, q, k, v, batch, seq, lambda_init):
    width = C_HEADS * 2 * C_HEAD_DIM
    q3, k3, v3 = (t.reshape(batch, seq, width) for t in (q, k, v))
    full = lambda a: pl.BlockSpec(a.shape, lambda b, h: (0, 0))
    spec = pl.BlockSpec((None, seq, LANES), lambda b, h: (b, 0, h))
    out = pl.pallas_call(
        functools.partial(_diff_attn_kernel, lambda_init=lambda_init),
        grid=(batch, C_HEADS),
        in_specs=[full(lam_params), full(subln), spec, spec, spec],
        out_specs=spec,
        out_shape=jax.ShapeDtypeStruct((batch, seq, width), BF16),
        scratch_shapes=[pltpu.VMEM((seq, 2 * LANES), BF16)],
        compiler_params=_params(2),
        name="diff_attention",
    )(lam_params, subln, q3, k3, v3)
    return out.reshape(batch * seq, width)


def _c_in_kernel(x_ref, w_ref, cos_ref, sin_ref, q_ref, k_ref, v_ref, *, tiles_per_group):
    upper = (_lane_iota() % C_HEAD_DIM) < (C_ROT // 2)
    width = q_ref.shape[1]
    h = _dot(x_ref[...].astype(BF16), w_ref[...])
    ca, sa = _rope_lane_tables(cos_ref[...], sin_ref[...], C_ROT, 0, C_HEAD_DIM,
                               _table_base(tiles_per_group))
    q_ref[...] = _rope_wide(h[:, :width], ca, sa, C_ROT // 2, upper).astype(BF16)
    k_ref[...] = _rope_wide(h[:, width:2 * width], ca, sa, C_ROT // 2, upper).astype(BF16)
    v_ref[...] = h[:, 2 * width:].astype(BF16)


def _c_in_proj(x2, w, cos_c, sin_c):
    m = x2.shape[0]
    tm = PROJ_ROWS
    width = w.shape[1] // 3
    row = lambda n: pl.BlockSpec((tm, n), lambda i: (i, 0))
    out = jax.ShapeDtypeStruct((m, width), BF16)
    tiles_per_group, table_spec = _table_spec(m, tm)
    return pl.pallas_call(
        functools.partial(_c_in_kernel, tiles_per_group=tiles_per_group),
        grid=(m // tm,),
        in_specs=[row(D_MODEL), pl.BlockSpec(w.shape, lambda i: (0, 0)), table_spec, table_spec],
        out_specs=(row(width), row(width), row(width)),
        out_shape=(out, out, out),
        compiler_params=_params(1),
        name="c_in_proj",
    )(x2, w, cos_c, sin_c)


def _gelu_tanh(c):
    return 0.5 * c * (1.0 + jnp.tanh(math.sqrt(2.0 / math.pi) * (c + 0.044715 * (c * c * c))))


def _layer_tail_kernel(*refs, n_in, tiles_per_seq):
    x_refs = refs[0:3]
    a_refs = refs[3:3 + 3 * n_in]
    w_refs = refs[3 + 3 * n_in:3 + 4 * n_in]
    (gm_ref, bm_ref, p_ref, wg_ref, wu_ref, cw_ref, cb_ref, wd_ref, g_ref, b_ref, wpg_ref,
     wpp_ref, o_ref, g_scr) = refs[3 + 4 * n_in:]
    tm = x_refs[0].shape[0]
    tn = FFN_CHUNK
    n_chunks = wg_ref.shape[1] // tn
    rows = tm + 2 * HALO
    pos = lax.rem(pl.program_id(0), tiles_per_seq)
    stack = lambda main, prev, nxt: jnp.concatenate([prev[...], main[...], nxt[...]], axis=0)

    y = None
    for k in range(n_in):
        d = _dot(stack(*a_refs[3 * k:3 * k + 3]), w_refs[k][...])
        y = d if y is None else y + d
    x1 = _layer_norm(ALPHA * stack(*x_refs) + y, gm_ref[...], bm_ref[...])
    r = lax.broadcasted_iota(jnp.int32, (rows, 1), 0)
    inside = ((r >= HALO) | (pos != 0)) & ((r < HALO + tm) | (pos != tiles_per_seq - 1))
    xh = jnp.where(inside, x1, 0.0).astype(BF16)
    xb = xh[HALO:HALO + tm]
    x = x1[HALO:HALO + tm]
    for n in range(n_chunks):
        cols = slice(n * tn, (n + 1) * tn)
        a = _dot(xh, wg_ref[:, cols])
        u = _dot(xb, wu_ref[:, cols])
        cw = cw_ref[:, cols]
        a_prev = pltpu.roll(a, 1, axis=0)[HALO:HALO + tm]
        a_next = pltpu.roll(a, rows - 1, axis=0)[HALO:HALO + tm]
        conv = (cb_ref[:, cols] + cw[0:1, :] * a_prev + cw[1:2, :] * a[HALO:HALO + tm]
                + cw[2:3, :] * a_next)
        g_scr[:, cols] = (_gelu_tanh(conv) * u).astype(BF16)
    f = _dot(g_scr[...], wd_ref[...])
    x2 = _layer_norm(ALPHA * x + f, g_ref[...], b_ref[...])
    gate = jax.nn.sigmoid(_dot(x2.astype(BF16), wpg_ref[...]))
    proj = _dot(p_ref[...].astype(BF16), wpp_ref[...])
    o_ref[...] = x2 + gate * proj


def _layer_tail(x2, acts, w_outs, gain_mix, bias_mix, p2, wg, wu, cw, cb, wd, gain, bias, wpg,
                wpp, seq):
    m = x2.shape[0]
    tm = FFN_ROWS
    halo_blocks = tm // HALO
    last_halo = m // HALO - 1

    def tiled(width):
        return [pl.BlockSpec((tm, width), lambda i: (i, 0)),
                pl.BlockSpec((HALO, width), lambda i: (jnp.maximum(i * halo_blocks - 1, 0), 0)),
                pl.BlockSpec((HALO, width),
                             lambda i: (jnp.minimum((i + 1) * halo_blocks, last_halo), 0))]

    def resident(a):
        nd = a.ndim
        return pl.BlockSpec(a.shape, lambda i: (0,) * nd, pipeline_mode=pl.Buffered(1))

    residents = [*w_outs, gain_mix, bias_mix]
    ffn_residents = [wg, wu, cw, cb, wd, gain, bias, wpg, wpp]
    in_specs = (tiled(D_MODEL) + [spec for a in acts for spec in tiled(a.shape[1])]
                + [resident(a) for a in residents]
                + [pl.BlockSpec((tm, PLE_DIM), lambda i: (i, 0))]
                + [resident(a) for a in ffn_residents])
    operands = ([x2] * 3 + [a for a in acts for _ in range(3)] + residents + [p2]
                + ffn_residents)
    return pl.pallas_call(
        functools.partial(_layer_tail_kernel, n_in=len(acts), tiles_per_seq=seq // tm),
        grid=(m // tm,),
        in_specs=in_specs,
        out_specs=pl.BlockSpec((tm, D_MODEL), lambda i: (i, 0)),
        out_shape=jax.ShapeDtypeStruct((m, D_MODEL), F32),
        scratch_shapes=[pltpu.VMEM((tm, D_FF), BF16)],
        compiler_params=_params(1),
        name="layer_tail",
    )(*operands)


def _compact_rope_tables(positions):
    inv = lambda rot: 1.0 / (ROPE_THETA ** (jnp.arange(0, rot, 2, dtype=F32) / rot))
    used = A_ROT // 2 + B_ROPE // 2
    inv_freq = jnp.concatenate(
        [inv(A_ROT), inv(B_ROPE), jnp.zeros((TABLE_GROUP_LANES - used,), F32)])
    pos = positions.reshape(TABLE_GROUPS, -1).astype(F32).T
    ang = (pos[:, :, None] * inv_freq[None, None, :]).reshape(pos.shape[0], LANES)
    return jnp.cos(ang), jnp.sin(ang)


def _ab_weights(w_in, w_q_up, w_kv_up):
    o3 = 3 * A_WIDTH
    o5 = o3 + B_Q_LORA + B_KV_LORA
    wmain = w_in[:, :o5].astype(BF16)
    wkpe = jnp.pad(w_in[:, o5:], ((0, 0), (B_NOPE, LANES - B_NOPE - B_ROPE))).astype(BF16)
    per_head = B_NOPE + B_ROPE
    wq = jnp.pad(w_q_up.reshape(B_Q_LORA, B_HEADS, per_head),
                 ((0, 0), (0, 0), (0, LANES - per_head))).reshape(B_Q_LORA, B_HEADS * LANES)
    kv = w_kv_up.reshape(B_KV_LORA, B_HEADS, B_NOPE + B_V)
    wk = jnp.pad(kv[:, :, :B_NOPE], ((0, 0), (0, 0), (0, LANES - B_NOPE)))
    wkv = jnp.concatenate([wk.reshape(B_KV_LORA, B_HEADS * LANES),
                           kv[:, :, B_NOPE:].reshape(B_KV_LORA, B_HEADS * B_V)], axis=1)
    return wmain, wkpe, wq.astype(BF16), wkv.astype(BF16)


def kernel(x, p, positions, ab_w_in, ab_q_norm, ab_w_q_up, ab_kv_norm, ab_w_kv_up, ab_w_out,
           c_w_qkv, c_lambda, c_subln, c_w_out, ln_mix_g, ln_mix_b, ffn_w_gate, ffn_w_up,
           ffn_conv_w, ffn_conv_b, ffn_w_down, ln_ffn_g, ln_ffn_b, ple_w_gate, ple_w_proj):
    batch, seq, d = x.shape
    m = batch * seq
    cos_c, sin_c = _compact_rope_tables(positions)
    x2 = x.reshape(m, d)
    row_vec = lambda v: v.reshape(1, -1)
    for i in range(DEPTH):
        j = i // 2
        if i % 2 == 0:
            wmain, wkpe, wq, wkv = _ab_weights(ab_w_in[j], ab_w_q_up[j], ab_w_kv_up[j])
            qa, ka, va, qb, kb, vb = _ab_in_proj(
                x2, wmain, wkpe, row_vec(ab_q_norm[j]), wq, row_vec(ab_kv_norm[j]), wkv, cos_c,
                sin_c)
            out_a = _dilated_attention(qa, ka, va, batch, seq)
            out_b = _mla_attention(qb, kb, vb, batch, seq)
            w_out = ab_w_out[j].astype(BF16)
            acts, weights = [out_a, out_b], [w_out[:A_WIDTH], w_out[A_WIDTH:]]
        else:
            lambda_init = 0.8 - 0.6 * math.exp(-0.3 * i)
            q, k, v = _c_in_proj(x2, c_w_qkv[j].astype(BF16), cos_c, sin_c)
            subln = row_vec(c_subln[j])
            acts = [_diff_attention(c_lambda[j], subln, q, k, v, batch, seq, lambda_init)]
            weights = [c_w_out[j].astype(BF16)]
        x2 = _layer_tail(x2, acts, weights, row_vec(ln_mix_g[i]), row_vec(ln_mix_b[i]),
                         p[i].reshape(m, PLE_DIM), ffn_w_gate[i].astype(BF16),
                         ffn_w_up[i].astype(BF16), ffn_conv_w[i], row_vec(ffn_conv_b[i]),
                         ffn_w_down[i].astype(BF16), row_vec(ln_ffn_g[i]), row_vec(ln_ffn_b[i]),
                         ple_w_gate[i].astype(BF16), ple_w_proj[i].astype(BF16), seq)
    return x2.reshape(batch, seq, d)
```

```python
import functools
import math

import numpy as np
import jax
import jax.numpy as jnp
from jax import lax
from jax.experimental import pallas as pl
from jax.experimental.pallas import tpu as pltpu

F32 = jnp.float32
BF16 = jnp.bfloat16

D_MODEL = 1024
DEPTH = 2
PLE_DIM = 256
ROPE_THETA = 500000.0
A_HEAD_DIM = 64
A_HEADS = 8
A_ROT = 16
A_PATTERNS = ((128, 1), (512, 4), (2048, 16))
A_SIDE = 64
B_HEADS = 8
B_Q_LORA = 384
B_KV_LORA = 256
B_NOPE = 64
B_ROPE = 32
B_V = 64
C_HEAD_DIM = 64
C_HEADS = 8
C_ROT = 16
D_FF = 2816
LN_EPS = 1e-5
RMS_EPS = 1e-6
NEG_INF = -1e30
ALPHA = (2 * DEPTH) ** 0.25
A_WIDTH = A_HEADS * A_HEAD_DIM
LOG2E = 1.4426950408889634
MLA_Q_SCALE = ((B_NOPE + B_ROPE) ** -0.5) * LOG2E
DIFF_Q_SCALE = (C_HEAD_DIM ** -0.5) * LOG2E

LANES = 128
TABLE_GROUPS = 4
TABLE_GROUP_LANES = LANES // TABLE_GROUPS
VMEM_LIMIT = 60 * 1024 * 1024

PROJ_ROWS = 512
FFN_ROWS = 1024
FFN_CHUNK = 256
HALO = 16
ATT_Q_ROWS = 256
ATT_UNROLL = 8
DIL_Q_ROWS = 128


def _params(n_axes):
    return pltpu.CompilerParams(
        dimension_semantics=("arbitrary",) * n_axes, vmem_limit_bytes=VMEM_LIMIT)


def _dot(a, b):
    return jnp.dot(a, b, preferred_element_type=F32)


def _dot_nt(a, b):
    return lax.dot_general(a, b, (((1,), (1,)), ((), ())), preferred_element_type=F32)


def _rope_group(x, cos_t, sin_t, half, take_upper):
    upper = pltpu.roll(x, LANES - half, axis=1)
    lower = pltpu.roll(x, half, axis=1)
    return x * cos_t + jnp.where(take_upper, upper, lower) * sin_t


def _rope_wide(x, cos_t, sin_t, half, take_upper):
    groups = [
        _rope_group(x[:, g * LANES:(g + 1) * LANES], cos_t, sin_t, half, take_upper)
        for g in range(x.shape[1] // LANES)
    ]
    return jnp.concatenate(groups, axis=1)


def _rope_lane_tables(cos_c, sin_c, rot, rot_start, period, base):
    rel = _lane_iota() % period - rot_start
    rotary = (rel >= 0) & (rel < rot)
    first = rel < rot // 2
    freq = jnp.where(first, rel, rel - rot // 2)
    idx = jnp.broadcast_to(jnp.where(rotary, base + freq, 0), cos_c.shape)
    cos_t = jnp.where(rotary, jnp.take_along_axis(cos_c, idx, axis=1), 1.0)
    sin_g = jnp.take_along_axis(sin_c, idx, axis=1)
    sin_t = jnp.where(rotary, jnp.where(first, -sin_g, sin_g), 0.0)
    return cos_t, sin_t


def _rms_norm(h, gain, eps):
    ms = jnp.mean(h * h, axis=-1, keepdims=True)
    return h * lax.rsqrt(ms + eps) * gain


def _layer_norm(z, gain, bias):
    mu = jnp.mean(z, axis=-1, keepdims=True)
    zc = z - mu
    var = jnp.mean(zc * zc, axis=-1, keepdims=True)
    return zc * lax.rsqrt(var + LN_EPS) * gain + bias


def _lane_iota():
    return lax.broadcasted_iota(jnp.int32, (1, LANES), 1)


def _softmax_pv(s, v_aug, c):
    m = jnp.max(s, axis=-1, keepdims=True)
    d = s - m
    p = jnp.exp2(d if c is None else d * c).astype(BF16)
    return _dot(p, v_aug), m


def _table_base(tiles_per_group):
    return lax.div(pl.program_id(0), tiles_per_group) * TABLE_GROUP_LANES


def _ab_in_kernel(x_ref, wmain_ref, wkpe_ref, qn_ref, wq_ref, kvn_ref, wkv_ref, cos_ref, sin_ref,
                  qa_ref, ka_ref, va_ref, qb_ref, kb_ref, vb_ref, *, tiles_per_group):
    lane = _lane_iota()
    upper_a = (lane % A_HEAD_DIM) < (A_ROT // 2)
    upper_b = (lane >= B_NOPE) & (lane < B_NOPE + B_ROPE // 2)
    xb = x_ref[...].astype(BF16)
    h = _dot(xb, wmain_ref[...])
    cos_c, sin_c = cos_ref[...], sin_ref[...]
    base = _table_base(tiles_per_group)
    ca, sa = _rope_lane_tables(cos_c, sin_c, A_ROT, 0, A_HEAD_DIM, base)
    cb, sb = _rope_lane_tables(cos_c, sin_c, B_ROPE, B_NOPE, LANES, base + A_ROT // 2)
    o1, o2, o3 = A_WIDTH, 2 * A_WIDTH, 3 * A_WIDTH
    o4 = o3 + B_Q_LORA
    o5 = o4 + B_KV_LORA
    qa_ref[...] = _rope_wide(h[:, :o1], ca, sa, A_ROT // 2, upper_a)
    ka_ref[...] = _rope_wide(h[:, o1:o2], ca, sa, A_ROT // 2, upper_a)
    va_ref[...] = h[:, o2:o3]
    cq = _rms_norm(h[:, o3:o4], qn_ref[...], RMS_EPS).astype(BF16)
    qb = _dot(cq, wq_ref[...])
    qb_ref[...] = (_rope_wide(qb, cb, sb, B_ROPE // 2, upper_b) * MLA_Q_SCALE).astype(BF16)
    ckv = _rms_norm(h[:, o4:o5], kvn_ref[...], RMS_EPS).astype(BF16)
    kv = _dot(ckv, wkv_ref[...])
    kpe = _rope_group(_dot(xb, wkpe_ref[...]), cb, sb, B_ROPE // 2, upper_b)
    kslots = B_HEADS * LANES
    kb_ref[...] = (kv[:, :kslots] + jnp.concatenate([kpe] * B_HEADS, axis=1)).astype(BF16)
    vb_ref[...] = kv[:, kslots:].astype(BF16)


def _table_spec(m, tm):
    group_rows = m // TABLE_GROUPS
    assert group_rows % tm == 0
    tiles_per_group = group_rows // tm
    return tiles_per_group, pl.BlockSpec((tm, LANES), lambda i: (i % tiles_per_group, 0))


def _ab_in_proj(x2, wmain, wkpe, qn, wq, kvn, wkv, cos_c, sin_c):
    m = x2.shape[0]
    tm = PROJ_ROWS
    row = lambda w: pl.BlockSpec((tm, w), lambda i: (i, 0))
    full = lambda a: pl.BlockSpec(a.shape, lambda i: (0, 0))
    out_shape = (
        jax.ShapeDtypeStruct((m, A_WIDTH), F32),
        jax.ShapeDtypeStruct((m, A_WIDTH), F32),
        jax.ShapeDtypeStruct((m, A_WIDTH), F32),
        jax.ShapeDtypeStruct((m, B_HEADS * LANES), BF16),
        jax.ShapeDtypeStruct((m, B_HEADS * LANES), BF16),
        jax.ShapeDtypeStruct((m, B_HEADS * B_V), BF16),
    )
    tiles_per_group, table_spec = _table_spec(m, tm)
    return pl.pallas_call(
        functools.partial(_ab_in_kernel, tiles_per_group=tiles_per_group),
        grid=(m // tm,),
        in_specs=[row(D_MODEL), full(wmain), full(wkpe), full(qn), full(wq), full(kvn), full(wkv),
                  table_spec, table_spec],
        out_specs=(row(A_WIDTH), row(A_WIDTH), row(A_WIDTH), row(B_HEADS * LANES),
                   row(B_HEADS * LANES), row(B_HEADS * B_V)),
        out_shape=out_shape,
        compiler_params=_params(1),
        name="ab_in_proj",
    )(x2, wmain, wkpe, qn, wq, kvn, wkv, cos_c, sin_c)


def _band_bias(tq, tk, offset):
    delta = np.arange(tk)[None, :] - np.arange(tq)[:, None] + offset
    bias = np.where(np.abs(delta) <= A_SIDE, 0.0, NEG_INF).astype(np.float32)
    return np.concatenate([bias, bias], axis=0)


def _window_start(q0, length, tk):
    return min(max(q0 - A_SIDE, 0), length - tk)


def _dilated_kernel(q_ref, k_ref, v_ref, bias_w_ref, bias_s_ref, o_ref, *stat_refs):
    seq = q_ref.shape[0]
    tq = DIL_Q_ROWS
    c = (A_HEAD_DIM ** -0.5) * LOG2E
    head0 = _lane_iota() < A_HEAD_DIM

    for pat, (window, dil) in enumerate(A_PATTERNS):
        assert window // (2 * dil) == A_SIDE
        num_ref, m_ref, l_ref = stat_refs[3 * pat:3 * pat + 3]
        length = seq // dil
        tk = min(tq + 2 * A_SIDE, length)
        ones = jnp.ones((tk, LANES), BF16)
        for res in range(dil):
            for q0 in range(0, length, tq):
                ws = _window_start(q0, length, tk)
                if dil == 1:
                    qrows, krows = pl.ds(q0, tq), pl.ds(ws, tk)
                else:
                    qrows = pl.ds(res + q0 * dil, tq, stride=dil)
                    krows = pl.ds(res + ws * dil, tk, stride=dil)
                if tk == length:
                    bias = bias_s_ref[...]
                else:
                    bias = bias_w_ref[(0, A_SIDE, 2 * A_SIDE).index(q0 - ws)]
                q = q_ref[qrows, :]
                ql = jnp.concatenate(
                    [jnp.where(head0, q, 0.0), jnp.where(head0, 0.0, q)], axis=0).astype(BF16)
                k = k_ref[krows, :].astype(BF16)
                v_aug = jnp.concatenate([v_ref[krows, :].astype(BF16), ones], axis=1)
                r, m = _softmax_pv(_dot_nt(ql, k) + bias, v_aug, c)
                num_ref[qrows, :] = jnp.where(head0, r[:tq, :LANES], r[tq:, :LANES])
                l_ref[qrows, :] = jnp.where(head0, r[:tq, LANES:], r[tq:, LANES:])
                m_ref[qrows, :] = jnp.where(head0, m[:tq], m[tq:])

    rows = 2 * tq
    n_pat = len(A_PATTERNS)
    for r0 in range(0, seq, rows):
        sl = pl.ds(r0, rows)
        ms = [stat_refs[3 * pat + 1][sl, :] for pat in range(n_pat)]
        m_all = functools.reduce(jnp.maximum, ms)
        scale = [jnp.exp2((mp - m_all) * c) for mp in ms]
        num = sum(stat_refs[3 * pat][sl, :] * scale[pat] for pat in range(n_pat))
        den = sum(stat_refs[3 * pat + 2][sl, :] * scale[pat] for pat in range(n_pat))
        o_ref[sl, :] = (num / den).astype(o_ref.dtype)


def _dilated_attention(qa, ka, va, batch, seq):
    tq = DIL_Q_ROWS
    q3, k3, v3 = (t.reshape(batch, seq, A_WIDTH) for t in (qa, ka, va))
    tkw = tq + 2 * A_SIDE
    bias_w = jnp.asarray(np.stack([_band_bias(tq, tkw, -off) for off in (0, A_SIDE, 2 * A_SIDE)]))
    bias_s = jnp.asarray(_band_bias(tq, tq, 0))
    spec = pl.BlockSpec((None, seq, LANES), lambda b, j: (b, 0, j))
    full = lambda a: pl.BlockSpec(a.shape, lambda b, j: (0,) * a.ndim)
    out = pl.pallas_call(
        _dilated_kernel,
        grid=(batch, A_WIDTH // LANES),
        in_specs=[spec, spec, spec, full(bias_w), full(bias_s)],
        out_specs=spec,
        out_shape=jax.ShapeDtypeStruct((batch, seq, A_WIDTH), BF16),
        scratch_shapes=[pltpu.VMEM((seq, LANES), F32)] * (3 * len(A_PATTERNS)),
        compiler_params=_params(2),
        name="dilated_attention",
    )(q3, k3, v3, bias_w, bias_s)
    return out.reshape(batch * seq, A_WIDTH)


def _mla_attn_kernel(q_ref, k_ref, v_ref, o_ref, vaug_ref):
    seq = q_ref.shape[0]
    tq = ATT_Q_ROWS
    head0 = _lane_iota() < B_V
    vaug_ref[:, :LANES] = v_ref[...]
    vaug_ref[:, LANES:] = jnp.ones((seq, LANES), BF16)

    def tile(i, carry):
        rows = pl.ds(pl.multiple_of(i * tq, tq), tq)
        outs = []
        for h in range(2):
            q = q_ref[rows, h * LANES:(h + 1) * LANES]
            k = k_ref[:, h * LANES:(h + 1) * LANES]
            r, _ = _softmax_pv(_dot_nt(q, k), vaug_ref[...], None)
            outs.append(r[:, :LANES] / r[:, LANES:])
        o_ref[rows, :] = jnp.where(head0, outs[0], outs[1]).astype(o_ref.dtype)
        return carry

    lax.fori_loop(0, seq // tq, tile, 0, unroll=ATT_UNROLL)


def _mla_attention(qb, kb, vb, batch, seq):
    q3 = qb.reshape(batch, seq, B_HEADS * LANES)
    k3 = kb.reshape(batch, seq, B_HEADS * LANES)
    v3 = vb.reshape(batch, seq, B_HEADS * B_V)
    qk_spec = pl.BlockSpec((None, seq, 2 * LANES), lambda b, j: (b, 0, j))
    v_spec = pl.BlockSpec((None, seq, LANES), lambda b, j: (b, 0, j))
    out = pl.pallas_call(
        _mla_attn_kernel,
        grid=(batch, B_HEADS // 2),
        in_specs=[qk_spec, qk_spec, v_spec],
        out_specs=v_spec,
        out_shape=jax.ShapeDtypeStruct((batch, seq, B_HEADS * B_V), BF16),
        scratch_shapes=[pltpu.VMEM((seq, 2 * LANES), BF16)],
        compiler_params=_params(2),
        name="mla_attention",
    )(q3, k3, v3)
    return out.reshape(batch * seq, B_HEADS * B_V)


def _diff_attn_kernel(lam_ref, subln_ref, q_ref, k_ref, v_ref, o_ref, vaug_ref, *, lambda_init):
    seq = q_ref.shape[0]
    tq = ATT_Q_ROWS
    lp = lam_ref[...]
    t1 = jnp.sum(lp[0:1, :] * lp[1:2, :], axis=-1, keepdims=True)
    t2 = jnp.sum(lp[2:3, :] * lp[3:4, :], axis=-1, keepdims=True)
    lam = jnp.exp(t1) - jnp.exp(t2) + lambda_init
    comp0 = _lane_iota() < C_HEAD_DIM
    gain = subln_ref[...] * (1.0 - lambda_init)
    vaug_ref[:, :LANES] = v_ref[...]
    vaug_ref[:, LANES:] = jnp.ones((seq, LANES), BF16)

    def tile(i, carry):
        rows = pl.ds(pl.multiple_of(i * tq, tq), tq)
        q = q_ref[rows, :]
        zero = jnp.zeros_like(q)
        ql = jnp.concatenate([jnp.where(comp0, q, zero), jnp.where(comp0, zero, q)], axis=0)
        r, _ = _softmax_pv(_dot_nt(ql, k_ref[...]), vaug_ref[...], None)
        o = r[:tq, :LANES] / r[:tq, LANES:] - lam * (r[tq:, :LANES] / r[tq:, LANES:])
        o_ref[rows, :] = _rms_norm(o, gain, LN_EPS).astype(o_ref.dtype)
        return carry

    lax.fori_loop(0, seq // tq, tile, 0, unroll=ATT_UNROLL)


def _diff_attention(lam_params, subln, q, k, v, batch, seq, lambda_init):
    width = C_HEADS * 2 * C_HEAD_DIM
    q3, k3, v3 = (t.reshape(batch, seq, width) for t in (q, k, v))
    full = lambda a: pl.BlockSpec(a.shape, lambda b, h: (0, 0))
    spec = pl.BlockSpec((None, seq, LANES), lambda b, h: (b, 0, h))
    out = pl.pallas_call(
        functools.partial(_diff_attn_kernel, lambda_init=lambda_init),
        grid=(batch, C_HEADS),
        in_specs=[full(lam_params), full(subln), spec, spec, spec],
        out_specs=spec,
        out_shape=jax.ShapeDtypeStruct((batch, seq, width), BF16),
        scratch_shapes=[pltpu.VMEM((seq, 2 * LANES), BF16)],
        compiler_params=_params(2),
        name="diff_attention",
    )(lam_params, subln, q3, k3, v3)
    return out.reshape(batch * seq, width)


def _c_in_kernel(x_ref, w_ref, cos_ref, sin_ref, q_ref, k_ref, v_ref, *, tiles_per_group):
    upper = (_lane_iota() % C_HEAD_DIM) < (C_ROT // 2)
    width = q_ref.shape[1]
    h = _dot(x_ref[...].astype(BF16), w_ref[...])
    ca, sa = _rope_lane_tables(cos_ref[...], sin_ref[...], C_ROT, 0, C_HEAD_DIM,
                               _table_base(tiles_per_group))
    q_ref[...] = (_rope_wide(h[:, :width], ca, sa, C_ROT // 2, upper) * DIFF_Q_SCALE).astype(BF16)
    k_ref[...] = _rope_wide(h[:, width:2 * width], ca, sa, C_ROT // 2, upper).astype(BF16)
    v_ref[...] = h[:, 2 * width:].astype(BF16)


def _c_in_proj(x2, w, cos_c, sin_c):
    m = x2.shape[0]
    tm = PROJ_ROWS
    width = w.shape[1] // 3
    row = lambda n: pl.BlockSpec((tm, n), lambda i: (i, 0))
    out = jax.ShapeDtypeStruct((m, width), BF16)
    tiles_per_group, table_spec = _table_spec(m, tm)
    return pl.pallas_call(
        functools.partial(_c_in_kernel, tiles_per_group=tiles_per_group),
        grid=(m // tm,),
        in_specs=[row(D_MODEL), pl.BlockSpec(w.shape, lambda i: (0, 0)), table_spec, table_spec],
        out_specs=(row(width), row(width), row(width)),
        out_shape=(out, out, out),
        compiler_params=_params(1),
        name="c_in_proj",
    )(x2, w, cos_c, sin_c)


def _gelu_tanh(c):
    return 0.5 * c * (1.0 + jnp.tanh(math.sqrt(2.0 / math.pi) * (c + 0.044715 * (c * c * c))))


def _layer_tail_kernel(*refs, n_in, tiles_per_seq):
    x_refs = refs[0:3]
    a_refs = refs[3:3 + 3 * n_in]
    w_refs = refs[3 + 3 * n_in:3 + 4 * n_in]
    (gm_ref, bm_ref, p_ref, wg_ref, wu_ref, cw_ref, cb_ref, wd_ref, g_ref, b_ref, wpg_ref,
     wpp_ref, o_ref, g_scr) = refs[3 + 4 * n_in:]
    tm = x_refs[0].shape[0]
    tn = FFN_CHUNK
    n_chunks = wg_ref.shape[1] // tn
    rows = tm + 2 * HALO
    pos = lax.rem(pl.program_id(0), tiles_per_seq)
    stack = lambda main, prev, nxt: jnp.concatenate([prev[...], main[...], nxt[...]], axis=0)

    y = None
    for k in range(n_in):
        d = _dot(stack(*a_refs[3 * k:3 * k + 3]), w_refs[k][...])
        y = d if y is None else y + d
    x1 = _layer_norm(ALPHA * stack(*x_refs) + y, gm_ref[...], bm_ref[...])
    r = lax.broadcasted_iota(jnp.int32, (rows, 1), 0)
    inside = ((r >= HALO) | (pos != 0)) & ((r < HALO + tm) | (pos != tiles_per_seq - 1))
    xh = jnp.where(inside, x1, 0.0).astype(BF16)
    xb = xh[HALO:HALO + tm]
    x = x1[HALO:HALO + tm]
    for n in range(n_chunks):
        cols = slice(n * tn, (n + 1) * tn)
        a = _dot(xh, wg_ref[:, cols])
        u = _dot(xb, wu_ref[:, cols])
        cw = cw_ref[:, cols]
        a_prev = pltpu.roll(a, 1, axis=0)[HALO:HALO + tm]
        a_next = pltpu.roll(a, rows - 1, axis=0)[HALO:HALO + tm]
        conv = (cb_ref[:, cols] + cw[0:1, :] * a_prev + cw[1:2, :] * a[HALO:HALO + tm]
                + cw[2:3, :] * a_next)
        g_scr[:, cols] = (_gelu_tanh(conv) * u).astype(BF16)
    f = _dot(g_scr[...], wd_ref[...])
    x2 = _layer_norm(ALPHA * x + f, g_ref[...], b_ref[...])
    gate = jax.nn.sigmoid(_dot(x2.astype(BF16), wpg_ref[...]))
    proj = _dot(p_ref[...].astype(BF16), wpp_ref[...])
    o_ref[...] = x2 + gate * proj


def _layer_tail(x2, acts, w_outs, gain_mix, bias_mix, p2, wg, wu, cw, cb, wd, gain, bias, wpg,
                wpp, seq):
    m = x2.shape[0]
    tm = FFN_ROWS
    halo_blocks = tm // HALO
    last_halo = m // HALO - 1

    def tiled(width):
        return [pl.BlockSpec((tm, width), lambda i: (i, 0)),
                pl.BlockSpec((HALO, width), lambda i: (jnp.maximum(i * halo_blocks - 1, 0), 0)),
                pl.BlockSpec((HALO, width),
                             lambda i: (jnp.minimum((i + 1) * halo_blocks, last_halo), 0))]

    def resident(a):
        nd = a.ndim
        return pl.BlockSpec(a.shape, lambda i: (0,) * nd, pipeline_mode=pl.Buffered(1))

    residents = [*w_outs, gain_mix, bias_mix]
    ffn_residents = [wg, wu, cw, cb, wd, gain, bias, wpg, wpp]
    in_specs = (tiled(D_MODEL) + [spec for a in acts for spec in tiled(a.shape[1])]
                + [resident(a) for a in residents]
                + [pl.BlockSpec((tm, PLE_DIM), lambda i: (i, 0))]
                + [resident(a) for a in ffn_residents])
    operands = ([x2] * 3 + [a for a in acts for _ in range(3)] + residents + [p2]
                + ffn_residents)
    return pl.pallas_call(
        functools.partial(_layer_tail_kernel, n_in=len(acts), tiles_per_seq=seq // tm),
        grid=(m // tm,),
        in_specs=in_specs,
        out_specs=pl.BlockSpec((tm, D_MODEL), lambda i: (i, 0)),
        out_shape=jax.ShapeDtypeStruct((m, D_MODEL), F32),
        scratch_shapes=[pltpu.VMEM((tm, D_FF), BF16)],
        compiler_params=_params(1),
        name="layer_tail",
    )(*operands)


def _compact_rope_tables(positions):
    inv = lambda rot: 1.0 / (ROPE_THETA ** (jnp.arange(0, rot, 2, dtype=F32) / rot))
    used = A_ROT // 2 + B_ROPE // 2
    inv_freq = jnp.concatenate(
        [inv(A_ROT), inv(B_ROPE), jnp.zeros((TABLE_GROUP_LANES - used,), F32)])
    pos = positions.reshape(TABLE_GROUPS, -1).astype(F32).T
    ang = (pos[:, :, None] * inv_freq[None, None, :]).reshape(pos.shape[0], LANES)
    return jnp.cos(ang), jnp.sin(ang)


def _ab_weights(w_in, w_q_up, w_kv_up):
    o3 = 3 * A_WIDTH
    o5 = o3 + B_Q_LORA + B_KV_LORA
    wmain = w_in[:, :o5].astype(BF16)
    wkpe = jnp.pad(w_in[:, o5:], ((0, 0), (B_NOPE, LANES - B_NOPE - B_ROPE))).astype(BF16)
    per_head = B_NOPE + B_ROPE
    wq = jnp.pad(w_q_up.reshape(B_Q_LORA, B_HEADS, per_head),
                 ((0, 0), (0, 0), (0, LANES - per_head))).reshape(B_Q_LORA, B_HEADS * LANES)
    kv = w_kv_up.reshape(B_KV_LORA, B_HEADS, B_NOPE + B_V)
    wk = jnp.pad(kv[:, :, :B_NOPE], ((0, 0), (0, 0), (0, LANES - B_NOPE)))
    wkv = jnp.concatenate([wk.reshape(B_KV_LORA, B_HEADS * LANES),
                           kv[:, :, B_NOPE:].reshape(B_KV_LORA, B_HEADS * B_V)], axis=1)
    return wmain, wkpe, wq.astype(BF16), wkv.astype(BF16)


def kernel(x, p, positions, ab_w_in, ab_q_norm, ab_w_q_up, ab_kv_norm, ab_w_kv_up, ab_w_out,
           c_w_qkv, c_lambda, c_subln, c_w_out, ln_mix_g, ln_mix_b, ffn_w_gate, ffn_w_up,
           ffn_conv_w, ffn_conv_b, ffn_w_down, ln_ffn_g, ln_ffn_b, ple_w_gate, ple_w_proj):
    batch, seq, d = x.shape
    m = batch * seq
    cos_c, sin_c = _compact_rope_tables(positions)
    x2 = x.reshape(m, d)
    row_vec = lambda v: v.reshape(1, -1)
    for i in range(DEPTH):
        j = i // 2
        if i % 2 == 0:
            wmain, wkpe, wq, wkv = _ab_weights(ab_w_in[j], ab_w_q_up[j], ab_w_kv_up[j])
            qa, ka, va, qb, kb, vb = _ab_in_proj(
                x2, wmain, wkpe, row_vec(ab_q_norm[j]), wq, row_vec(ab_kv_norm[j]), wkv, cos_c,
                sin_c)
            out_a = _dilated_attention(qa, ka, va, batch, seq)
            out_b = _mla_attention(qb, kb, vb, batch, seq)
            w_out = ab_w_out[j].astype(BF16)
            acts, weights = [out_a, out_b], [w_out[:A_WIDTH], w_out[A_WIDTH:]]
        else:
            lambda_init = 0.8 - 0.6 * math.exp(-0.3 * i)
            q, k, v = _c_in_proj(x2, c_w_qkv[j].astype(BF16), cos_c, sin_c)
            subln = row_vec(c_subln[j])
            acts = [_diff_attention(c_lambda[j], subln, q, k, v, batch, seq, lambda_init)]
            weights = [c_w_out[j].astype(BF16)]
        x2 = _layer_tail(x2, acts, weights, row_vec(ln_mix_g[i]), row_vec(ln_mix_b[i]),
                         p[i].reshape(m, PLE_DIM), ffn_w_gate[i].astype(BF16),
                         ffn_w_up[i].astype(BF16), ffn_conv_w[i], row_vec(ffn_conv_b[i]),
                         ffn_w_down[i].astype(BF16), row_vec(ln_ffn_g[i]), row_vec(ln_ffn_b[i]),
                         ple_w_gate[i].astype(BF16), ple_w_proj[i].astype(BF16), seq)
    return x2.reshape(batch, seq, d)
```
